```python
import math
import jax, jax.numpy as jnp
from jax import lax
import numpy as np

D_MODEL = 1024
BATCH = 8
SEQ = 8192
DEPTH = 4
DEC_BATCH = 4
DEC_SEQ = 4096
PAST_LEN = 128

HEAD_DIM = 64
GRID_W = 64
Q_BLOCK = 128
EPS = 1e-6
A_HEADS = 8
A_KV_HEADS = 2
AXIAL_THETA = 10000.0
B_HEADS = 4
B_V_DIM = 2 * HEAD_DIM
C_HEADS = 16
C_KV_HEADS = 4
WINDOW = 128
ROPE_THETA = 500000.0
ROPE_DIM = HEAD_DIM // 4
FFN_HIDDEN = -(-8 * D_MODEL // (3 * 256)) * 256

A_Q = A_HEADS * HEAD_DIM
A_KV = A_KV_HEADS * HEAD_DIM
B_QK = B_HEADS * 2 * HEAD_DIM
B_V = B_HEADS * B_V_DIM
EVEN_IN = A_Q + 2 * A_KV + 2 * B_QK + B_V
EVEN_OUT = A_Q + B_V
C_Q = C_HEADS * HEAD_DIM
C_KV = C_KV_HEADS * HEAD_DIM
ODD_IN = C_Q + 2 * C_KV
ODD_OUT = C_Q
N_EVEN = (DEPTH + 1) // 2
N_ODD = DEPTH // 2

kernel_name = "hybrid_axial_diff_window_encoder"


def rms_norm(x, g):
    xf = x.astype(jnp.float32)
    y = xf * lax.rsqrt(jnp.mean(xf * xf, axis=-1, keepdims=True) + EPS)
    return (y * g.astype(jnp.float32)).astype(x.dtype)


def rms_norm_nogain(x):
    xf = x.astype(jnp.float32)
    return (xf * lax.rsqrt(jnp.mean(xf * xf, axis=-1, keepdims=True) + EPS)).astype(x.dtype)


def rope_table(pos, dim, theta):
    inv = theta ** (-jnp.arange(0, dim, 2, dtype=jnp.float32) / dim)
    ang = pos.astype(jnp.float32)[:, None] * inv[None, :]
    return jnp.cos(ang), jnp.sin(ang)


def rotate(x, cos, sin):
    xf = x.astype(jnp.float32)
    half = xf.shape[-1] // 2
    x1, x2 = xf[..., :half], xf[..., half:]
    c, s = cos[:, None, :], sin[:, None, :]
    return jnp.concatenate([x1 * c - x2 * s, x2 * c + x1 * s], axis=-1).astype(x.dtype)


def partial_rope(x, cos, sin):
    return jnp.concatenate([rotate(x[..., :ROPE_DIM], cos, sin), x[..., ROPE_DIM:]], axis=-1)


def axial_rope(x, row_cos, row_sin, col_cos, col_sin):
    half = HEAD_DIM // 2
    return jnp.concatenate([rotate(x[..., :half], row_cos, row_sin),
                            rotate(x[..., half:], col_cos, col_sin)], axis=-1)


def to_blocks(q):
    B, S = q.shape[0], q.shape[1]
    qb = q.reshape((B, S // Q_BLOCK, Q_BLOCK) + q.shape[2:])
    return jnp.moveaxis(qb, 1, 0)


def from_blocks(o):
    o = jnp.moveaxis(o, 0, 1)
    return o.reshape((o.shape[0], o.shape[1] * o.shape[2]) + o.shape[3:])


def dense_gqa_blocks(q, k, v):
    B, S, Hq, d = q.shape
    Hkv = k.shape[2]
    G = Hq // Hkv
    scale = d ** -0.5
    qb = to_blocks(q.reshape(B, S, Hkv, G, d))

    def one_block(qi):
        s = jnp.einsum('bqhgd,bkhd->bhgqk', qi, k).astype(jnp.float32) * scale
        p = jax.nn.softmax(s, axis=-1).astype(v.dtype)
        return jnp.einsum('bhgqk,bkhd->bqhgd', p, v)

    return from_blocks(lax.map(one_block, qb)).reshape(B, S, Hq, d)


def diff_attention_blocks(qs, ks, v, lam):
    d = qs.shape[-1]
    scale = d ** -0.5
    qb = to_blocks(qs)

    def one_block(qi):
        s = jnp.einsum('bqmhd,bkmhd->bmhqk', qi, ks).astype(jnp.float32) * scale
        p = jax.nn.softmax(s, axis=-1)
        a = (p[:, 0] - lam * p[:, 1]).astype(v.dtype)
        return jnp.einsum('bhqk,bkhe->bqhe', a, v)

    return from_blocks(lax.map(one_block, qb))


def window_gqa_sink(q, k, v, sink):
    B, S, Hq, d = q.shape
    Hkv = k.shape[2]
    G = Hq // Hkv
    nb = S // Q_BLOCK
    KB = Q_BLOCK + 2 * WINDOW
    scale = d ** -0.5
    kp = jnp.pad(k, ((0, 0), (WINDOW, WINDOW), (0, 0), (0, 0)))
    vp = jnp.pad(v, ((0, 0), (WINDOW, WINDOW), (0, 0), (0, 0)))
    qb = to_blocks(q.reshape(B, S, Hkv, G, d))
    rel = jnp.arange(KB)[None, :] - WINDOW - jnp.arange(Q_BLOCK)[:, None]
    band = jnp.abs(rel) <= WINDOW
    sink_g = sink.reshape(Hkv, G).astype(jnp.float32)

    def one_block(args):
        i, qi = args
        start = i * Q_BLOCK
        ki = lax.dynamic_slice_in_dim(kp, start, KB, axis=1)
        vi = lax.dynamic_slice_in_dim(vp, start, KB, axis=1)
        key_pos = start - WINDOW + jnp.arange(KB)
        valid = band & ((key_pos >= 0) & (key_pos < S))[None, :]
        s = jnp.einsum('bqhgd,bkhd->bhgqk', qi, ki).astype(jnp.float32) * scale
        s = jnp.where(valid, s, -jnp.inf)
        sink_col = jnp.broadcast_to(sink_g[None, :, :, None, None], s.shape[:-1] + (1,))
        p = jax.nn.softmax(jnp.concatenate([s, sink_col], axis=-1), axis=-1)[..., :KB]
        return jnp.einsum('bhgqk,bkhd->bqhgd', p.astype(v.dtype), vi)

    o = lax.map(one_block, (jnp.arange(nb), qb))
    return from_blocks(o).reshape(B, S, Hq, d)


def even_mixer(h, w_in, w_out, qk_norm_a, diff_lambda, lam_init, axial, rope):
    B, S, _ = h.shape
    proj = h @ w_in
    qa, ka, va, qbd, kbd, vbd = jnp.split(
        proj, [A_Q, A_Q + A_KV, A_Q + 2 * A_KV, A_Q + 2 * A_KV + B_QK, A_Q + 2 * A_KV + 2 * B_QK], axis=-1)
    qa = rms_norm(qa.reshape(B, S, A_HEADS, HEAD_DIM), qk_norm_a[0])
    ka = rms_norm(ka.reshape(B, S, A_KV_HEADS, HEAD_DIM), qk_norm_a[1])
    va = va.reshape(B, S, A_KV_HEADS, HEAD_DIM)
    qa = axial_rope(qa, *axial)
    ka = axial_rope(ka, *axial)
    oa = dense_gqa_blocks(qa, ka, va).reshape(B, S, A_Q)
    cos, sin = rope
    qbd = partial_rope(qbd.reshape(B, S, B_HEADS * 2, HEAD_DIM), cos, sin)
    kbd = partial_rope(kbd.reshape(B, S, B_HEADS * 2, HEAD_DIM), cos, sin)
    qs = qbd.reshape(B, S, B_HEADS, 2, HEAD_DIM).swapaxes(2, 3)
    ks = kbd.reshape(B, S, B_HEADS, 2, HEAD_DIM).swapaxes(2, 3)
    vbd = vbd.reshape(B, S, B_HEADS, B_V_DIM)
    lf = diff_lambda.astype(jnp.float32)
    lam = jnp.exp(jnp.sum(lf[0] * lf[1])) - jnp.exp(jnp.sum(lf[2] * lf[3])) + lam_init
    ob = diff_attention_blocks(qs, ks, vbd, lam)
    ob = (rms_norm_nogain(ob) * (1.0 - lam_init)).reshape(B, S, B_V)
    return jnp.concatenate([oa, ob], axis=-1) @ w_out


def odd_mixer(h, w_in, w_out, sink, rope):
    B, S, _ = h.shape
    cos, sin = rope
    q, k, v = jnp.split(h @ w_in, [C_Q, C_Q + C_KV], axis=-1)
    q = partial_rope(q.reshape(B, S, C_HEADS, HEAD_DIM), cos, sin)
    k = partial_rope(k.reshape(B, S, C_KV_HEADS, HEAD_DIM), cos, sin)
    v = v.reshape(B, S, C_KV_HEADS, HEAD_DIM)
    o = window_gqa_sink(q, k, v, sink).reshape(B, S, ODD_OUT)
    return o @ w_out


def swiglu(h, w_gate_up, w_down):
    g, u = jnp.split(h @ w_gate_up, 2, axis=-1)
    return (jax.nn.silu(g) * u) @ w_down


def trunk(x, w_in_even, w_out_even, qk_norm_a, diff_lambda, w_in_odd, w_out_odd, sink_c,
          w_gate_up, w_down, norm_mix_pre, norm_mix_post, norm_ffn_pre, norm_ffn_post):
    S = x.shape[1]
    rows = S // GRID_W
    t_row = jnp.broadcast_to(jnp.arange(rows)[:, None], (rows, GRID_W)).reshape(-1)
    t_col = jnp.broadcast_to(jnp.arange(GRID_W)[None, :], (rows, GRID_W)).reshape(-1)
    row_cos, row_sin = rope_table(t_row, HEAD_DIM // 2, AXIAL_THETA)
    col_cos, col_sin = rope_table(t_col, HEAD_DIM // 2, AXIAL_THETA)
    axial = (row_cos, row_sin, col_cos, col_sin)
    rope = rope_table(jnp.arange(S), ROPE_DIM, ROPE_THETA)
    for l in range(DEPTH):
        h = rms_norm(x, norm_mix_pre[l])
        if l % 2 == 0:
            lam_init = 0.8 - 0.6 * math.exp(-0.3 * l)
            m = even_mixer(h, w_in_even[l // 2], w_out_even[l // 2], qk_norm_a[l // 2],
                           diff_lambda[l // 2], lam_init, axial, rope)
        else:
            m = odd_mixer(h, w_in_odd[l // 2], w_out_odd[l // 2], sink_c[l // 2], rope)
        x = x + rms_norm(m, norm_mix_post[l])
        h = rms_norm(x, norm_ffn_pre[l])
        x = x + rms_norm(swiglu(h, w_gate_up[l], w_down[l]), norm_ffn_post[l])
    return x


def setup_inputs(seed: int = 0) -> dict:
    key = jax.random.key(seed)
    ks = jax.random.split(key, 16)
    f32 = jnp.float32

    def nrm(k, shape, scale):
        return jax.random.normal(k, shape, f32) * scale

    def gain(k, shape):
        return jnp.ones(shape, f32) + 0.05 * jax.random.normal(k, shape, f32)

    return {
        "x_prompt": nrm(ks[0], (BATCH, SEQ, D_MODEL), 1.0),
        "x_sample": nrm(ks[1], (DEC_BATCH, DEC_SEQ, D_MODEL), 1.0),
        "w_in_even": nrm(ks[2], (N_EVEN, D_MODEL, EVEN_IN), D_MODEL ** -0.5),
        "w_out_even": nrm(ks[3], (N_EVEN, EVEN_OUT, D_MODEL), EVEN_OUT ** -0.5),
        "qk_norm_a": gain(ks[4], (N_EVEN, 2, HEAD_DIM)),
        "diff_lambda": nrm(ks[5], (N_EVEN, 4, HEAD_DIM), 0.1),
        "w_in_odd": nrm(ks[6], (N_ODD, D_MODEL, ODD_IN), D_MODEL ** -0.5),
        "w_out_odd": nrm(ks[7], (N_ODD, ODD_OUT, D_MODEL), ODD_OUT ** -0.5),
        "sink_c": nrm(ks[8], (N_ODD, C_HEADS), 0.5),
        "w_gate_up": nrm(ks[9], (DEPTH, D_MODEL, 2 * FFN_HIDDEN), D_MODEL ** -0.5),
        "w_down": nrm(ks[10], (DEPTH, FFN_HIDDEN, D_MODEL), FFN_HIDDEN ** -0.5),
        "norm_mix_pre": gain(ks[11], (DEPTH, D_MODEL)),
        "norm_mix_post": gain(ks[12], (DEPTH, D_MODEL)),
        "norm_ffn_pre": gain(ks[13], (DEPTH, D_MODEL)),
        "norm_ffn_post": gain(ks[14], (DEPTH, D_MODEL)),
    }


def reference(x_prompt, x_sample, w_in_even, w_out_even, qk_norm_a, diff_lambda, w_in_odd,
              w_out_odd, sink_c, w_gate_up, w_down, norm_mix_pre, norm_mix_post, norm_ffn_pre,
              norm_ffn_post):
    y_prompt = trunk(x_prompt, w_in_even, w_out_even, qk_norm_a, diff_lambda, w_in_odd, w_out_odd,
                     sink_c, w_gate_up, w_down, norm_mix_pre, norm_mix_post, norm_ffn_pre, norm_ffn_post)
    y_sample = trunk(x_sample, w_in_even, w_out_even, qk_norm_a, diff_lambda, w_in_odd, w_out_odd,
                     sink_c, w_gate_up, w_down, norm_mix_pre, norm_mix_post, norm_ffn_pre, norm_ffn_post)
    return (y_prompt, y_sample)
```

```python
import functools
import math

import jax
import jax.numpy as jnp
from jax import lax
from jax.experimental import pallas as pl
from jax.experimental.pallas import tpu as pltpu

F32 = jnp.float32
BF16 = jnp.bfloat16

D_MODEL = 1024
HEAD_DIM = 64
GRID_W = 64
EPS = 1e-6
A_HEADS = 8
A_KV_HEADS = 2
AXIAL_THETA = 10000.0
B_HEADS = 4
B_V_DIM = 2 * HEAD_DIM
C_HEADS = 16
C_KV_HEADS = 4
WINDOW = 128
ROPE_THETA = 500000.0
ROPE_DIM = HEAD_DIM // 4
FFN_HIDDEN = 2816

A_Q = A_HEADS * HEAD_DIM
A_KV = A_KV_HEADS * HEAD_DIM
B_QK = B_HEADS * 2 * HEAD_DIM
B_V = B_HEADS * B_V_DIM
EVEN_IN = A_Q + 2 * A_KV + 2 * B_QK + B_V
C_Q = C_HEADS * HEAD_DIM
C_KV = C_KV_HEADS * HEAD_DIM
ODD_IN = C_Q + 2 * C_KV

LOG2E = 1.4426950408889634
Q_SCALE = HEAD_DIM ** -0.5 * LOG2E

TOKEN_TILE = 512
Q_TILE = 256
KEY_CHUNK = 512
FFN_CHUNK = 256
VMEM_LIMIT = 56 * 1024 * 1024

_TAB_ROWS = 80


def _rope_table(pos, dim, theta):
    inv = theta ** (-jnp.arange(0, dim, 2, dtype=F32) / dim)
    ang = pos.astype(F32)[:, None] * inv[None, :]
    return jnp.cos(ang), jnp.sin(ang)


def _position_table(seq):
    rows = seq // GRID_W
    t_row = jnp.broadcast_to(jnp.arange(rows)[:, None], (rows, GRID_W)).reshape(-1)
    t_col = jnp.broadcast_to(jnp.arange(GRID_W)[None, :], (rows, GRID_W)).reshape(-1)
    row_cos, row_sin = _rope_table(t_row, HEAD_DIM // 2, AXIAL_THETA)
    col_cos, col_sin = _rope_table(t_col, HEAD_DIM // 2, AXIAL_THETA)
    cos, sin = _rope_table(jnp.arange(seq), ROPE_DIM, ROPE_THETA)
    return jnp.concatenate([row_cos, row_sin, col_cos, col_sin, cos, sin], axis=1).T


def _rms(x, g):
    return x * lax.rsqrt(jnp.mean(x * x, axis=-1, keepdims=True) + EPS) * g


def _axial_rope_t(t, tab):
    rc, rs, cc, cs = tab[0:16][None], tab[16:32][None], tab[32:48][None], tab[48:64][None]
    x1, x2, x3, x4 = t[:, 0:16], t[:, 16:32], t[:, 32:48], t[:, 48:64]
    return jnp.concatenate(
        [x1 * rc - x2 * rs, x2 * rc + x1 * rs, x3 * cc - x4 * cs, x4 * cc + x3 * cs], axis=1)


def _partial_rope_t(t, tab):
    c, s = tab[64:72][None], tab[72:80][None]
    x1, x2 = t[:, 0:8], t[:, 8:16]
    return jnp.concatenate([x1 * c - x2 * s, x2 * c + x1 * s, t[:, 16:]], axis=1)


def _head_rms_t(t, g):
    r = lax.rsqrt(jnp.mean(t * t, axis=1, keepdims=True) + EPS)
    return t * r * g[None]


def _proj_t(wt_ref, lo, hi, h):
    return lax.dot_general(wt_ref[lo:hi, :], h, (((1,), (1,)), ((), ())),
                           preferred_element_type=F32)


def _even_in_kernel(x_ref, g_ref, wt_ref, tab_ref, qkg_ref,
                    qa_ref, ka_ref, va_ref, qb_ref, kb_ref, vb_ref):
    n = x_ref.shape[0]
    h = _rms(x_ref[...], g_ref[...]).astype(BF16)
    tab = tab_ref[...]
    o = 0
    t = _proj_t(wt_ref, o, o + A_Q, h).reshape(A_HEADS, HEAD_DIM, n)
    t = _axial_rope_t(_head_rms_t(t, qkg_ref[0]), tab) * Q_SCALE
    qa_ref[...] = t.reshape(A_Q, n).astype(BF16)
    o += A_Q
    t = _proj_t(wt_ref, o, o + A_KV, h).reshape(A_KV_HEADS, HEAD_DIM, n)
    t = _axial_rope_t(_head_rms_t(t, qkg_ref[1]), tab)
    ka_ref[...] = t.reshape(A_KV, n).T.astype(BF16)
    o += A_KV
    va_ref[...] = _proj_t(wt_ref, o, o + A_KV, h).astype(BF16)
    o += A_KV
    t = _proj_t(wt_ref, o, o + B_QK, h).reshape(2 * B_HEADS, HEAD_DIM, n)
    qb_ref[...] = (_partial_rope_t(t, tab) * Q_SCALE).reshape(B_QK, n).astype(BF16)
    o += B_QK
    t = _proj_t(wt_ref, o, o + B_QK, h).reshape(2 * B_HEADS, HEAD_DIM, n)
    kb_ref[...] = _partial_rope_t(t, tab).reshape(B_QK, n).T.astype(BF16)
    o += B_QK
    vb_ref[...] = _proj_t(wt_ref, o, o + B_V, h).astype(BF16)


def _odd_in_kernel(x_ref, g_ref, wt_ref, tab_ref, q_ref, k_ref, v_ref):
    n = x_ref.shape[0]
    h = _rms(x_ref[...], g_ref[...]).astype(BF16)
    tab = tab_ref[...]
    t = _proj_t(wt_ref, 0, C_Q, h).reshape(C_HEADS, HEAD_DIM, n)
    q_ref[...] = (_partial_rope_t(t, tab) * Q_SCALE).reshape(C_Q, n).astype(BF16)
    t = _proj_t(wt_ref, C_Q, C_Q + C_KV, h).reshape(C_KV_HEADS, HEAD_DIM, n)
    k_ref[...] = _partial_rope_t(t, tab).reshape(C_KV, n).T.astype(BF16)
    v_ref[...] = _proj_t(wt_ref, C_Q + C_KV, ODD_IN, h).astype(BF16)


def _resident(shape):
    nd = len(shape)
    return pl.BlockSpec(shape, lambda *_: (0,) * nd, pipeline_mode=pl.Buffered(1))


def _token_params():
    return pltpu.CompilerParams(dimension_semantics=("parallel",), vmem_limit_bytes=VMEM_LIMIT)


def _input_projection(x, gain, wt, tab, pos_tile, qk_gain=None):
    tokens = x.shape[0]
    tm = TOKEN_TILE
    grid = (tokens // tm,)
    in_specs = [
        pl.BlockSpec((tm, D_MODEL), lambda i: (i, 0)),
        _resident((1, D_MODEL)),
        _resident(wt.shape),
        pl.BlockSpec((_TAB_ROWS, tm), lambda i: (0, pos_tile(i))),
    ]
    args = [x, gain.reshape(1, D_MODEL), wt, tab]

    def feat(rows):
        return (jax.ShapeDtypeStruct((rows, tokens), BF16), pl.BlockSpec((rows, tm), lambda i: (0, i)))

    def tokm(cols):
        return (jax.ShapeDtypeStruct((tokens, cols), BF16), pl.BlockSpec((tm, cols), lambda i: (i, 0)))

    if qk_gain is not None:
        in_specs.append(_resident((2, HEAD_DIM, 1)))
        args.append(qk_gain.reshape(2, HEAD_DIM, 1))
        outs = [feat(A_Q), tokm(A_KV), feat(A_KV), feat(B_QK), tokm(B_QK), feat(B_V)]
        body, name = _even_in_kernel, "even_in_proj"
    else:
        outs = [feat(C_Q), tokm(C_KV), feat(C_KV)]
        body, name = _odd_in_kernel, "odd_in_proj"
    return pl.pallas_call(
        body, grid=grid, in_specs=in_specs,
        out_specs=[o[1] for o in outs], out_shape=[o[0] for o in outs],
        compiler_params=_token_params(), name=name)(*args)


def _flash_t(k_ref, vt_ref, qext, key_chunk):
    seq = k_ref.shape[0]
    dv = vt_ref.shape[0]
    n = qext.shape[1]

    def body(c, carry):
        m, l, acc = carry
        off = pl.multiple_of(c * key_chunk, key_chunk)
        s = jnp.dot(k_ref[pl.ds(off, key_chunk), :], qext, preferred_element_type=F32)
        m_new = jnp.maximum(m, jnp.max(s, axis=0, keepdims=True))
        alpha = jnp.exp2(m - m_new)
        p = jnp.exp2(s - m_new)
        l = alpha * l + jnp.sum(p, axis=0, keepdims=True)
        pv = jnp.dot(vt_ref[:, pl.ds(off, key_chunk)], p.astype(BF16), preferred_element_type=F32)
        return m_new, l, alpha * acc + pv

    init = (jnp.full((1, n), -jnp.inf, F32), jnp.zeros((1, n), F32), jnp.zeros((dv, n), F32))
    return lax.fori_loop(0, seq // key_chunk, body, init)


def _select_rows(q, idx, count):
    zero = jnp.zeros_like(q)
    return jnp.concatenate([jnp.where(idx == r, q, zero) for r in range(count)], axis=0)


def _heads_to_token_major(o, heads, tq):
    stacked = jnp.concatenate([o[:, g * tq:(g + 1) * tq] for g in range(heads)], axis=0)
    return stacked.T


def _attn_a_kernel(q_ref, k_ref, vt_ref, o_ref, *, key_chunk):
    tq = q_ref.shape[1]
    group = A_HEADS // A_KV_HEADS
    j = pl.program_id(1)
    q = q_ref[...]
    qext = jnp.concatenate(
        [_select_rows(q[g * HEAD_DIM:(g + 1) * HEAD_DIM], j, A_KV_HEADS) for g in range(group)], axis=1)
    _, l, acc = _flash_t(k_ref, vt_ref, qext, key_chunk)
    o_ref[...] = _heads_to_token_major(acc / l, group, tq).astype(BF16)


def _attn_b_kernel(q_ref, k_ref, vt_ref, lam_ref, o_ref, *, key_chunk, lam_init):
    tq = q_ref.shape[1]
    q = q_ref[...]
    zero = jnp.zeros((HEAD_DIM, tq), BF16)
    qext = jnp.concatenate(
        [jnp.concatenate([q[:HEAD_DIM], zero], axis=0), jnp.concatenate([zero, q[HEAD_DIM:]], axis=0)],
        axis=1)
    _, l, acc = _flash_t(k_ref, vt_ref, qext, key_chunk)
    lf = lam_ref[...]
    lam = (jnp.exp(jnp.sum(lf[0:1] * lf[1:2], axis=1, keepdims=True))
           - jnp.exp(jnp.sum(lf[2:3] * lf[3:4], axis=1, keepdims=True)) + lam_init)
    o = acc / l
    o = o[:, :tq] - lam * o[:, tq:]
    o = o * lax.rsqrt(jnp.mean(o * o, axis=0, keepdims=True) + EPS) * (1.0 - lam_init)
    o_ref[...] = o.T.astype(BF16)


def _attn_c_kernel(sink_ref, q_ref, k_ref, vt_ref, o_ref):
    tq = q_ref.shape[1]
    seq = k_ref.shape[0]
    group = C_HEADS // C_KV_HEADS
    span = tq + 2 * WINDOW
    j = pl.program_id(1)
    q0 = pl.program_id(2) * tq
    start = pl.multiple_of(jnp.clip(q0 - WINDOW, 0, seq - span), WINDOW)
    q = q_ref[...]
    qext = jnp.concatenate(
        [_select_rows(q[g * HEAD_DIM:(g + 1) * HEAD_DIM], j, C_KV_HEADS) for g in range(group)], axis=1)
    s = jnp.dot(k_ref[pl.ds(start, span), :], qext, preferred_element_type=F32)
    key_pos = start + lax.broadcasted_iota(jnp.int32, (span, tq), 0)
    qry_pos = q0 + lax.broadcasted_iota(jnp.int32, (span, tq), 1)
    valid = jnp.abs(key_pos - qry_pos) <= WINDOW
    valid = jnp.concatenate([valid] * group, axis=1)
    s = jnp.where(valid, s, -jnp.inf)
    sink = jnp.concatenate(
        [jnp.full((1, tq), sink_ref[j * group + g], F32) for g in range(group)], axis=1) * LOG2E
    m = jnp.maximum(sink, jnp.max(s, axis=0, keepdims=True))
    p = jnp.exp2(s - m)
    l = jnp.exp2(sink - m) + jnp.sum(p, axis=0, keepdims=True)
    acc = jnp.dot(vt_ref[:, pl.ds(start, span)], p.astype(BF16), preferred_element_type=F32)
    o_ref[...] = _heads_to_token_major(acc / l, group, tq).astype(BF16)


def _attn_params():
    return pltpu.CompilerParams(dimension_semantics=("parallel", "parallel", "arbitrary"),
                                vmem_limit_bytes=VMEM_LIMIT)


def _attention_a(qt, k, vt, batch, seq, tok0):
    tq = Q_TILE
    nq = seq // tq
    qb0, sb0 = tok0 // tq, tok0 // seq
    width = (A_HEADS // A_KV_HEADS) * HEAD_DIM
    return pl.pallas_call(
        functools.partial(_attn_a_kernel, key_chunk=min(KEY_CHUNK, seq)),
        grid=(batch, A_KV_HEADS, nq),
        in_specs=[
            pl.BlockSpec((width, tq), lambda b, j, i: (j, qb0 + b * nq + i)),
            pl.BlockSpec((seq, A_KV), lambda b, j, i: (sb0 + b, 0)),
            pl.BlockSpec((HEAD_DIM, seq), lambda b, j, i: (j, sb0 + b)),
        ],
        out_specs=pl.BlockSpec((tq, width), lambda b, j, i: (b * nq + i, j)),
        out_shape=jax.ShapeDtypeStruct((batch * seq, A_Q), BF16),
        compiler_params=_attn_params(), name="attn_axial")(qt, k, vt)


def _attention_b(qt, k, vt, lam_w, lam_init, batch, seq, tok0):
    tq = Q_TILE
    nq = seq // tq
    qb0, sb0 = tok0 // tq, tok0 // seq
    return pl.pallas_call(
        functools.partial(_attn_b_kernel, key_chunk=min(KEY_CHUNK, seq), lam_init=lam_init),
        grid=(batch, B_HEADS, nq),
        in_specs=[
            pl.BlockSpec((2 * HEAD_DIM, tq), lambda b, j, i: (j, qb0 + b * nq + i)),
            pl.BlockSpec((seq, 2 * HEAD_DIM), lambda b, j, i: (sb0 + b, j)),
            pl.BlockSpec((B_V_DIM, seq), lambda b, j, i: (j, sb0 + b)),
            pl.BlockSpec((4, HEAD_DIM), lambda b, j, i: (0, 0)),
        ],
        out_specs=pl.BlockSpec((tq, B_V_DIM), lambda b, j, i: (b * nq + i, j)),
        out_shape=jax.ShapeDtypeStruct((batch * seq, B_V), BF16),
        compiler_params=_attn_params(), name="attn_diff")(qt, k, vt, lam_w)


def _attention_c(qt, k, vt, sink, batch, seq, tok0):
    tq = Q_TILE
    nq = seq // tq
    qb0, sb0 = tok0 // tq, tok0 // seq
    width = (C_HEADS // C_KV_HEADS) * HEAD_DIM
    return pl.pallas_call(
        _attn_c_kernel,
        grid=(batch, C_KV_HEADS, nq),
        in_specs=[
            pl.BlockSpec(memory_space=pltpu.SMEM),
            pl.BlockSpec((width, tq), lambda b, j, i: (j, qb0 + b * nq + i)),
            pl.BlockSpec((seq, C_KV), lambda b, j, i: (sb0 + b, 0)),
            pl.BlockSpec((HEAD_DIM, seq), lambda b, j, i: (j, sb0 + b)),
        ],
        out_specs=pl.BlockSpec((tq, width), lambda b, j, i: (b * nq + i, j)),
        out_shape=jax.ShapeDtypeStruct((batch * seq, C_Q), BF16),
        compiler_params=_attn_params(), name="attn_window")(sink, qt, k, vt)


def _post_kernel(lo_ref, hi_ref, x_ref, wo_ref, gmp_ref, gfp_ref, wgu_ref, wd_ref, gfo_ref, y_ref):
    half = lo_ref.shape[1]
    mix = (jnp.dot(lo_ref[...], wo_ref[0:half, :], preferred_element_type=F32)
           + jnp.dot(hi_ref[...], wo_ref[half:, :], preferred_element_type=F32))
    x = x_ref[...] + _rms(mix, gmp_ref[...])
    h = _rms(x, gfp_ref[...]).astype(BF16)
    acc = jnp.zeros(x.shape, F32)
    for c in range(FFN_HIDDEN // FFN_CHUNK):
        lo = c * FFN_CHUNK
        g = jnp.dot(h, wgu_ref[:, lo:lo + FFN_CHUNK], preferred_element_type=F32)
        u = jnp.dot(h, wgu_ref[:, FFN_HIDDEN + lo:FFN_HIDDEN + lo + FFN_CHUNK], preferred_element_type=F32)
        a = (g * (1.0 / (1.0 + jnp.exp(-g))) * u).astype(BF16)
        acc = acc + jnp.dot(a, wd_ref[lo:lo + FFN_CHUNK, :], preferred_element_type=F32)
    y_ref[...] = x + _rms(acc, gfo_ref[...])


def _post_attention(o_lo, o_hi, lo_blk, hi_blk, x, w_out, g_mix_post, g_ffn_pre, w_gu, w_down, g_ffn_post):
    tokens = x.shape[0]
    tm = TOKEN_TILE
    half = D_MODEL // 2
    row = lambda v: v.reshape(1, D_MODEL)
    return pl.pallas_call(
        _post_kernel, grid=(tokens // tm,),
        in_specs=[
            pl.BlockSpec((tm, half), lambda i: (i, lo_blk)),
            pl.BlockSpec((tm, half), lambda i: (i, hi_blk)),
            pl.BlockSpec((tm, D_MODEL), lambda i: (i, 0)),
            _resident(w_out.shape), _resident((1, D_MODEL)), _resident((1, D_MODEL)),
            _resident(w_gu.shape), _resident(w_down.shape), _resident((1, D_MODEL)),
        ],
        out_specs=pl.BlockSpec((tm, D_MODEL), lambda i: (i, 0)),
        out_shape=jax.ShapeDtypeStruct((tokens, D_MODEL), F32),
        compiler_params=_token_params(), name="out_proj_ffn")(
            o_lo, o_hi, x, w_out, row(g_mix_post), row(g_ffn_pre), w_gu, w_down, row(g_ffn_post))


def kernel(x_prompt, x_sample, w_in_even, w_out_even, qk_norm_a, diff_lambda, w_in_odd, w_out_odd,
           sink_c, w_gate_up, w_down, norm_mix_pre, norm_mix_post, norm_ffn_pre, norm_ffn_post):
    bp, sp, _ = x_prompt.shape
    bs, ss, _ = x_sample.shape
    groups = ((bp, sp, 0), (bs, ss, bp * sp))
    depth = w_gate_up.shape[0]
    tm = TOKEN_TILE
    assert sp % tm == 0 and ss % tm == 0 and (bp * sp) % ss == 0
    assert min(sp, ss) >= Q_TILE + 2 * WINDOW

    x = jnp.concatenate([x_prompt.reshape(bp * sp, D_MODEL), x_sample.reshape(bs * ss, D_MODEL)], axis=0)
    tab = _position_table(max(sp, ss))
    n_prompt_tiles = bp * sp // tm

    def pos_tile(i):
        return jnp.where(i < n_prompt_tiles, i % (sp // tm), (i - n_prompt_tiles) % (ss // tm))

    for l in range(depth):
        e = l // 2
        if l % 2 == 0:
            lam_init = 0.8 - 0.6 * math.exp(-0.3 * l)
            wt = w_in_even[e].T.astype(BF16)
            qa, ka, va, qb, kb, vb = _input_projection(x, norm_mix_pre[l], wt, tab, pos_tile, qk_norm_a[e])
            oa = jnp.concatenate([_attention_a(qa, ka, va, b, s, t0) for b, s, t0 in groups], axis=0)
            ob = jnp.concatenate(
                [_attention_b(qb, kb, vb, diff_lambda[e], lam_init, b, s, t0) for b, s, t0 in groups], axis=0)
            o_lo, o_hi, lo_blk, hi_blk = oa, ob, 0, 0
            w_out = w_out_even[e]
        else:
            wt = w_in_odd[e].T.astype(BF16)
            q, k, v = _input_projection(x, norm_mix_pre[l], wt, tab, pos_tile)
            o = jnp.concatenate([_attention_c(q, k, v, sink_c[e], b, s, t0) for b, s, t0 in groups], axis=0)
            o_lo, o_hi, lo_blk, hi_blk = o, o, 0, 1
            w_out = w_out_odd[e]
        x = _post_attention(o_lo, o_hi, lo_blk, hi_blk, x, w_out.astype(BF16), norm_mix_post[l],
                            norm_ffn_pre[l], w_gate_up[l].astype(BF16), w_down[l].astype(BF16),
                            norm_ffn_post[l])

    y_prompt = x[:bp * sp].reshape(bp, sp, D_MODEL)
    y_sample = x[bp * sp:].reshape(bs, ss, D_MODEL)
    return (y_prompt, y_sample)
```

```python
import functools
import math

import jax
import jax.numpy as jnp
from jax import lax
from jax.experimental import pallas as pl
from jax.experimental.pallas import tpu as pltpu

F32 = jnp.float32
BF16 = jnp.bfloat16

D_MODEL = 1024
HEAD_DIM = 64
GRID_W = 64
EPS = 1e-6
A_HEADS = 8
A_KV_HEADS = 2
AXIAL_THETA = 10000.0
B_HEADS = 4
B_V_DIM = 2 * HEAD_DIM
C_HEADS = 16
C_KV_HEADS = 4
WINDOW = 128
ROPE_THETA = 500000.0
ROPE_DIM = HEAD_DIM // 4
FFN_HIDDEN = 2816

A_Q = A_HEADS * HEAD_DIM
A_KV = A_KV_HEADS * HEAD_DIM
B_QK = B_HEADS * 2 * HEAD_DIM
B_V = B_HEADS * B_V_DIM
EVEN_IN = A_Q + 2 * A_KV + 2 * B_QK + B_V
C_Q = C_HEADS * HEAD_DIM
C_KV = C_KV_HEADS * HEAD_DIM
ODD_IN = C_Q + 2 * C_KV

LOG2E = 1.4426950408889634
Q_SCALE = HEAD_DIM ** -0.5 * LOG2E

TOKEN_TILE = 512
Q_TILE = 256
KEY_CHUNK = 512
FFN_CHUNK = 256
VMEM_LIMIT = 56 * 1024 * 1024

_TAB_ROWS = 80


def _rope_table(pos, dim, theta):
    inv = theta ** (-jnp.arange(0, dim, 2, dtype=F32) / dim)
    ang = pos.astype(F32)[:, None] * inv[None, :]
    return jnp.cos(ang), jnp.sin(ang)


def _position_table(seq):
    rows = seq // GRID_W
    t_row = jnp.broadcast_to(jnp.arange(rows)[:, None], (rows, GRID_W)).reshape(-1)
    t_col = jnp.broadcast_to(jnp.arange(GRID_W)[None, :], (rows, GRID_W)).reshape(-1)
    row_cos, row_sin = _rope_table(t_row, HEAD_DIM // 2, AXIAL_THETA)
    col_cos, col_sin = _rope_table(t_col, HEAD_DIM // 2, AXIAL_THETA)
    cos, sin = _rope_table(jnp.arange(seq), ROPE_DIM, ROPE_THETA)
    return jnp.concatenate([row_cos, row_sin, col_cos, col_sin, cos, sin], axis=1).T


def _rms(x, g):
    return x * lax.rsqrt(jnp.mean(x * x, axis=-1, keepdims=True) + EPS) * g


def _axial_rope_t(t, tab):
    rc, rs, cc, cs = tab[0:16][None], tab[16:32][None], tab[32:48][None], tab[48:64][None]
    x1, x2, x3, x4 = t[:, 0:16], t[:, 16:32], t[:, 32:48], t[:, 48:64]
    return jnp.concatenate(
        [x1 * rc - x2 * rs, x2 * rc + x1 * rs, x3 * cc - x4 * cs, x4 * cc + x3 * cs], axis=1)


def _partial_rope_t(t, tab):
    c, s = tab[64:72][None], tab[72:80][None]
    x1, x2 = t[:, 0:8], t[:, 8:16]
    return jnp.concatenate([x1 * c - x2 * s, x2 * c + x1 * s, t[:, 16:]], axis=1)


def _head_rms_t(t, g):
    r = lax.rsqrt(jnp.mean(t * t, axis=1, keepdims=True) + EPS)
    return t * r * g[None]


def _proj_t(wt_ref, lo, hi, h):
    return lax.dot_general(wt_ref[lo:hi, :], h, (((1,), (1,)), ((), ())),
                           preferred_element_type=F32)


def _even_in_kernel(x_ref, g_ref, wt_ref, tab_ref, qkg_ref,
                    qa_ref, ka_ref, va_ref, qb_ref, kb_ref, vb_ref):
    n = x_ref.shape[0]
    h = _rms(x_ref[...], g_ref[...]).astype(BF16)
    tab = tab_ref[...]
    o = 0
    t = _proj_t(wt_ref, o, o + A_Q, h).reshape(A_HEADS, HEAD_DIM, n)
    t = _axial_rope_t(_head_rms_t(t, qkg_ref[0]), tab) * Q_SCALE
    qa_ref[...] = t.reshape(A_Q, n).astype(BF16)
    o += A_Q
    t = _proj_t(wt_ref, o, o + A_KV, h).reshape(A_KV_HEADS, HEAD_DIM, n)
    t = _axial_rope_t(_head_rms_t(t, qkg_ref[1]), tab)
    ka_ref[...] = t.reshape(A_KV, n).T.astype(BF16)
    o += A_KV
    va_ref[...] = _proj_t(wt_ref, o, o + A_KV, h).astype(BF16)
    o += A_KV
    t = _proj_t(wt_ref, o, o + B_QK, h).reshape(2 * B_HEADS, HEAD_DIM, n)
    qb_ref[...] = (_partial_rope_t(t, tab) * Q_SCALE).reshape(B_QK, n).astype(BF16)
    o += B_QK
    t = _proj_t(wt_ref, o, o + B_QK, h).reshape(2 * B_HEADS, HEAD_DIM, n)
    kb_ref[...] = _partial_rope_t(t, tab).reshape(B_QK, n).T.astype(BF16)
    o += B_QK
    vb_ref[...] = _proj_t(wt_ref, o, o + B_V, h).astype(BF16)


def _odd_in_kernel(x_ref, g_ref, wt_ref, tab_ref, q_ref, k_ref, v_ref):
    n = x_ref.shape[0]
    h = _rms(x_ref[...], g_ref[...]).astype(BF16)
    tab = tab_ref[...]
    t = _proj_t(wt_ref, 0, C_Q, h).reshape(C_HEADS, HEAD_DIM, n)
    q_ref[...] = (_partial_rope_t(t, tab) * Q_SCALE).reshape(C_Q, n).astype(BF16)
    t = _proj_t(wt_ref, C_Q, C_Q + C_KV, h).reshape(C_KV_HEADS, HEAD_DIM, n)
    k_ref[...] = _partial_rope_t(t, tab).reshape(C_KV, n).T.astype(BF16)
    v_ref[...] = _proj_t(wt_ref, C_Q + C_KV, ODD_IN, h).astype(BF16)


def _resident(shape):
    nd = len(shape)
    return pl.BlockSpec(shape, lambda *_: (0,) * nd, pipeline_mode=pl.Buffered(1))


def _token_params():
    return pltpu.CompilerParams(dimension_semantics=("parallel",), vmem_limit_bytes=VMEM_LIMIT)


def _input_projection(x, gain, wt, tab, pos_tile, qk_gain=None):
    tokens = x.shape[0]
    tm = TOKEN_TILE
    grid = (tokens // tm,)
    in_specs = [
        pl.BlockSpec((tm, D_MODEL), lambda i: (i, 0)),
        _resident((1, D_MODEL)),
        _resident(wt.shape),
        pl.BlockSpec((_TAB_ROWS, tm), lambda i: (0, pos_tile(i))),
    ]
    args = [x, gain.reshape(1, D_MODEL), wt, tab]

    def feat(rows):
        return (jax.ShapeDtypeStruct((rows, tokens), BF16), pl.BlockSpec((rows, tm), lambda i: (0, i)))

    def tokm(cols):
        return (jax.ShapeDtypeStruct((tokens, cols), BF16), pl.BlockSpec((tm, cols), lambda i: (i, 0)))

    if qk_gain is not None:
        in_specs.append(_resident((2, HEAD_DIM, 1)))
        args.append(qk_gain.reshape(2, HEAD_DIM, 1))
        outs = [feat(A_Q), tokm(A_KV), feat(A_KV), feat(B_QK), tokm(B_QK), feat(B_V)]
        body, name = _even_in_kernel, "even_in_proj"
    else:
        outs = [feat(C_Q), tokm(C_KV), feat(C_KV)]
        body, name = _odd_in_kernel, "odd_in_proj"
    return pl.pallas_call(
        body, grid=grid, in_specs=in_specs,
        out_specs=[o[1] for o in outs], out_shape=[o[0] for o in outs],
        compiler_params=_token_params(), name=name)(*args)


def _flash_t(k_ref, vt_ref, qext, s_a, s_b, acc_ref, key_chunk):
    n = qext.shape[1]
    nc = k_ref.shape[0] // key_chunk
    assert nc >= 2 and nc % 2 == 0

    def scores(c, s_ref):
        off = pl.multiple_of(c * key_chunk, key_chunk)
        s = jnp.dot(k_ref[pl.ds(off, key_chunk), :], qext, preferred_element_type=F32)
        s_ref[...] = s
        return jnp.max(s, axis=0, keepdims=True)

    def accumulate(c, s_ref, mc, m, l):
        off = pl.multiple_of(c * key_chunk, key_chunk)
        m_new = jnp.maximum(m, mc)
        alpha = jnp.exp2(m - m_new)
        p = jnp.exp2(s_ref[...] - m_new)
        l = alpha * l + jnp.sum(p, axis=0, keepdims=True)
        pv = jnp.dot(vt_ref[:, pl.ds(off, key_chunk)], p.astype(BF16), preferred_element_type=F32)
        acc_ref[...] = alpha * acc_ref[...] + pv
        return m_new, l

    def body(i, carry):
        mc_a, m, l = carry
        mc_b = scores(2 * i + 1, s_b)
        m, l = accumulate(2 * i, s_a, mc_a, m, l)
        mc_a = scores(2 * i + 2, s_a)
        m, l = accumulate(2 * i + 1, s_b, mc_b, m, l)
        return mc_a, m, l

    acc_ref[...] = jnp.zeros(acc_ref.shape, F32)
    init = (scores(0, s_a), jnp.full((1, n), -jnp.inf, F32), jnp.zeros((1, n), F32))
    mc_a, m, l = lax.fori_loop(0, nc // 2 - 1, body, init)
    mc_b = scores(nc - 1, s_b)
    m, l = accumulate(nc - 2, s_a, mc_a, m, l)
    m, l = accumulate(nc - 1, s_b, mc_b, m, l)
    return l, acc_ref[...]


def _select_rows(q, idx, count):
    zero = jnp.zeros_like(q)
    return jnp.concatenate([jnp.where(idx == r, q, zero) for r in range(count)], axis=0)


def _heads_to_token_major(o, heads, tq):
    stacked = jnp.concatenate([o[:, g * tq:(g + 1) * tq] for g in range(heads)], axis=0)
    return stacked.T


def _attn_a_kernel(q_ref, k_ref, vt_ref, o_ref, s_a, s_b, acc_ref, *, key_chunk):
    tq = q_ref.shape[1]
    group = A_HEADS // A_KV_HEADS
    j = pl.program_id(1)
    q = q_ref[...]
    qext = jnp.concatenate(
        [_select_rows(q[g * HEAD_DIM:(g + 1) * HEAD_DIM], j, A_KV_HEADS) for g in range(group)], axis=1)
    l, acc = _flash_t(k_ref, vt_ref, qext, s_a, s_b, acc_ref, key_chunk)
    o_ref[...] = _heads_to_token_major(acc / l, group, tq).astype(BF16)


def _attn_b_kernel(q_ref, k_ref, vt_ref, lam_ref, o_ref, s_a, s_b, acc_ref, *, key_chunk, lam_init):
    tq = q_ref.shape[1]
    q = q_ref[...]
    zero = jnp.zeros((HEAD_DIM, tq), BF16)
    qext = jnp.concatenate(
        [jnp.concatenate([q[:HEAD_DIM], zero], axis=0), jnp.concatenate([zero, q[HEAD_DIM:]], axis=0)],
        axis=1)
    l, acc = _flash_t(k_ref, vt_ref, qext, s_a, s_b, acc_ref, key_chunk)
    lf = lam_ref[...]
    lam = (jnp.exp(jnp.sum(lf[0:1] * lf[1:2], axis=1, keepdims=True))
           - jnp.exp(jnp.sum(lf[2:3] * lf[3:4], axis=1, keepdims=True)) + lam_init)
    o = acc / l
    o = o[:, :tq] - lam * o[:, tq:]
    o = o * lax.rsqrt(jnp.mean(o * o, axis=0, keepdims=True) + EPS) * (1.0 - lam_init)
    o_ref[...] = o.T.astype(BF16)


def _attn_c_kernel(sink_ref, q_ref, k_ref, vt_ref, o_ref):
    tq = q_ref.shape[1]
    seq = k_ref.shape[0]
    group = C_HEADS // C_KV_HEADS
    span = tq + 2 * WINDOW
    j = pl.program_id(1)
    q0 = pl.program_id(2) * tq
    start = pl.multiple_of(jnp.clip(q0 - WINDOW, 0, seq - span), WINDOW)
    q = q_ref[...]
    qext = jnp.concatenate(
        [_select_rows(q[g * HEAD_DIM:(g + 1) * HEAD_DIM], j, C_KV_HEADS) for g in range(group)], axis=1)
    s = jnp.dot(k_ref[pl.ds(start, span), :], qext, preferred_element_type=F32)
    key_pos = start + lax.broadcasted_iota(jnp.int32, (span, tq), 0)
    qry_pos = q0 + lax.broadcasted_iota(jnp.int32, (span, tq), 1)
    valid = jnp.abs(key_pos - qry_pos) <= WINDOW
    valid = jnp.concatenate([valid] * group, axis=1)
    s = jnp.where(valid, s, -jnp.inf)
    sink = jnp.concatenate(
        [jnp.full((1, tq), sink_ref[j * group + g], F32) for g in range(group)], axis=1) * LOG2E
    m = jnp.maximum(sink, jnp.max(s, axis=0, keepdims=True))
    p = jnp.exp2(s - m)
    l = jnp.exp2(sink - m) + jnp.sum(p, axis=0, keepdims=True)
    acc = jnp.dot(vt_ref[:, pl.ds(start, span)], p.astype(BF16), preferred_element_type=F32)
    o_ref[...] = _heads_to_token_major(acc / l, group, tq).astype(BF16)


def _attn_params():
    return pltpu.CompilerParams(dimension_semantics=("parallel", "parallel", "arbitrary"),
                                vmem_limit_bytes=VMEM_LIMIT)


def _flash_scratch(key_chunk, n, dv):
    return [pltpu.VMEM((key_chunk, n), F32), pltpu.VMEM((key_chunk, n), F32), pltpu.VMEM((dv, n), F32)]


def _attention_a(qt, k, vt, batch, seq, tok0):
    tq = Q_TILE
    nq = seq // tq
    qb0, sb0 = tok0 // tq, tok0 // seq
    group = A_HEADS // A_KV_HEADS
    width = group * HEAD_DIM
    key_chunk = min(KEY_CHUNK, seq // 2)
    return pl.pallas_call(
        functools.partial(_attn_a_kernel, key_chunk=key_chunk),
        grid=(batch, A_KV_HEADS, nq),
        scratch_shapes=_flash_scratch(key_chunk, group * tq, HEAD_DIM),
        in_specs=[
            pl.BlockSpec((width, tq), lambda b, j, i: (j, qb0 + b * nq + i)),
            pl.BlockSpec((seq, A_KV), lambda b, j, i: (sb0 + b, 0)),
            pl.BlockSpec((HEAD_DIM, seq), lambda b, j, i: (j, sb0 + b)),
        ],
        out_specs=pl.BlockSpec((tq, width), lambda b, j, i: (b * nq + i, j)),
        out_shape=jax.ShapeDtypeStruct((batch * seq, A_Q), BF16),
        compiler_params=_attn_params(), name="attn_axial")(qt, k, vt)


def _attention_b(qt, k, vt, lam_w, lam_init, batch, seq, tok0):
    tq = 2 * Q_TILE
    nq = seq // tq
    qb0, sb0 = tok0 // tq, tok0 // seq
    key_chunk = min(KEY_CHUNK, seq // 2)
    return pl.pallas_call(
        functools.partial(_attn_b_kernel, key_chunk=key_chunk, lam_init=lam_init),
        grid=(batch, B_HEADS, nq),
        scratch_shapes=_flash_scratch(key_chunk, 2 * tq, B_V_DIM),
        in_specs=[
            pl.BlockSpec((2 * HEAD_DIM, tq), lambda b, j, i: (j, qb0 + b * nq + i)),
            pl.BlockSpec((seq, 2 * HEAD_DIM), lambda b, j, i: (sb0 + b, j)),
            pl.BlockSpec((B_V_DIM, seq), lambda b, j, i: (j, sb0 + b)),
            pl.BlockSpec((4, HEAD_DIM), lambda b, j, i: (0, 0)),
        ],
        out_specs=pl.BlockSpec((tq, B_V_DIM), lambda b, j, i: (b * nq + i, j)),
        out_shape=jax.ShapeDtypeStruct((batch * seq, B_V), BF16),
        compiler_params=_attn_params(), name="attn_diff")(qt, k, vt, lam_w)


def _attention_c(qt, k, vt, sink, batch, seq, tok0):
    tq = Q_TILE
    nq = seq // tq
    qb0, sb0 = tok0 // tq, tok0 // seq
    width = (C_HEADS // C_KV_HEADS) * HEAD_DIM
    return pl.pallas_call(
        _attn_c_kernel,
        grid=(batch, C_KV_HEADS, nq),
        in_specs=[
            pl.BlockSpec(memory_space=pltpu.SMEM),
            pl.BlockSpec((width, tq), lambda b, j, i: (j, qb0 + b * nq + i)),
            pl.BlockSpec((seq, C_KV), lambda b, j, i: (sb0 + b, 0)),
            pl.BlockSpec((HEAD_DIM, seq), lambda b, j, i: (j, sb0 + b)),
        ],
        out_specs=pl.BlockSpec((tq, width), lambda b, j, i: (b * nq + i, j)),
        out_shape=jax.ShapeDtypeStruct((batch * seq, C_Q), BF16),
        compiler_params=_attn_params(), name="attn_window")(sink, qt, k, vt)


def _post_kernel(lo_ref, hi_ref, x_ref, wo_ref, gmp_ref, gfp_ref, wgu_ref, wd_ref, gfo_ref, y_ref):
    half = lo_ref.shape[1]
    mix = (jnp.dot(lo_ref[...], wo_ref[0:half, :], preferred_element_type=F32)
           + jnp.dot(hi_ref[...], wo_ref[half:, :], preferred_element_type=F32))
    x = x_ref[...] + _rms(mix, gmp_ref[...])
    h = _rms(x, gfp_ref[...]).astype(BF16)
    acc = jnp.zeros(x.shape, F32)
    for c in range(FFN_HIDDEN // FFN_CHUNK):
        lo = c * FFN_CHUNK
        g = jnp.dot(h, wgu_ref[:, lo:lo + FFN_CHUNK], preferred_element_type=F32)
        u = jnp.dot(h, wgu_ref[:, FFN_HIDDEN + lo:FFN_HIDDEN + lo + FFN_CHUNK], preferred_element_type=F32)
        a = (g * (1.0 / (1.0 + jnp.exp(-g))) * u).astype(BF16)
        acc = acc + jnp.dot(a, wd_ref[lo:lo + FFN_CHUNK, :], preferred_element_type=F32)
    y_ref[...] = x + _rms(acc, gfo_ref[...])


def _post_attention(o_lo, o_hi, lo_blk, hi_blk, x, w_out, g_mix_post, g_ffn_pre, w_gu, w_down, g_ffn_post):
    tokens = x.shape[0]
    tm = TOKEN_TILE
    half = D_MODEL // 2
    row = lambda v: v.reshape(1, D_MODEL)
    return pl.pallas_call(
        _post_kernel, grid=(tokens // tm,),
        in_specs=[
            pl.BlockSpec((tm, half), lambda i: (i, lo_blk)),
            pl.BlockSpec((tm, half), lambda i: (i, hi_blk)),
            pl.BlockSpec((tm, D_MODEL), lambda i: (i, 0)),
            _resident(w_out.shape), _resident((1, D_MODEL)), _resident((1, D_MODEL)),
            _resident(w_gu.shape), _resident(w_down.shape), _resident((1, D_MODEL)),
        ],
        out_specs=pl.BlockSpec((tm, D_MODEL), lambda i: (i, 0)),
        out_shape=jax.ShapeDtypeStruct((tokens, D_MODEL), F32),
        compiler_params=_token_params(), name="out_proj_ffn")(
            o_lo, o_hi, x, w_out, row(g_mix_post), row(g_ffn_pre), w_gu, w_down, row(g_ffn_post))


def kernel(x_prompt, x_sample, w_in_even, w_out_even, qk_norm_a, diff_lambda, w_in_odd, w_out_odd,
           sink_c, w_gate_up, w_down, norm_mix_pre, norm_mix_post, norm_ffn_pre, norm_ffn_post):
    bp, sp, _ = x_prompt.shape
    bs, ss, _ = x_sample.shape
    groups = ((bp, sp, 0), (bs, ss, bp * sp))
    depth = w_gate_up.shape[0]
    tm = TOKEN_TILE
    assert sp % tm == 0 and ss % tm == 0 and (bp * sp) % ss == 0
    assert min(sp, ss) >= Q_TILE + 2 * WINDOW

    x = jnp.concatenate([x_prompt.reshape(bp * sp, D_MODEL), x_sample.reshape(bs * ss, D_MODEL)], axis=0)
    tab = _position_table(max(sp, ss))
    n_prompt_tiles = bp * sp // tm

    def pos_tile(i):
        return jnp.where(i < n_prompt_tiles, i % (sp // tm), (i - n_prompt_tiles) % (ss // tm))

    for l in range(depth):
        e = l // 2
        if l % 2 == 0:
            lam_init = 0.8 - 0.6 * math.exp(-0.3 * l)
            wt = w_in_even[e].T.astype(BF16)
            qa, ka, va, qb, kb, vb = _input_projection(x, norm_mix_pre[l], wt, tab, pos_tile, qk_norm_a[e])
            oa = jnp.concatenate([_attention_a(qa, ka, va, b, s, t0) for b, s, t0 in groups], axis=0)
            ob = jnp.concatenate(
                [_attention_b(qb, kb, vb, diff_lambda[e], lam_init, b, s, t0) for b, s, t0 in groups], axis=0)
            o_lo, o_hi, lo_blk, hi_blk = oa, ob, 0, 0
            w_out = w_out_even[e]
        else:
            wt = w_in_odd[e].T.astype(BF16)
            q, k, v = _input_projection(x, norm_mix_pre[l], wt, tab, pos_tile)
            o = jnp.concatenate([_attention_c(q, k, v, sink_c[e], b, s, t0) for b, s, t0 in groups], axis=0)
            o_lo, o_hi, lo_blk, hi_blk = o, o, 0, 1
            w_out = w_out_odd[e]
        x = _post_attention(o_lo, o_hi, lo_blk, hi_blk, x, w_out.astype(BF16), norm_mix_post[l],
                            norm_ffn_pre[l], w_gate_up[l].astype(BF16), w_down[l].astype(BF16),
                            norm_ffn_post[l])

    y_prompt = x[:bp * sp].reshape(bp, sp, D_MODEL)
    y_sample = x[bp * sp:].reshape(bs, ss, D_MODEL)
    return (y_prompt, y_sample)
```

```python
import functools
import math

import jax
import jax.numpy as jnp
from jax import lax
from jax.experimental import pallas as pl
from jax.experimental.pallas import tpu as pltpu

F32 = jnp.float32
BF16 = jnp.bfloat16

D_MODEL = 1024
HEAD_DIM = 64
GRID_W = 64
EPS = 1e-6
A_HEADS = 8
A_KV_HEADS = 2
AXIAL_THETA = 10000.0
B_HEADS = 4
B_V_DIM = 2 * HEAD_DIM
C_HEADS = 16
C_KV_HEADS = 4
WINDOW = 128
ROPE_THETA = 500000.0
ROPE_DIM = HEAD_DIM // 4
FFN_HIDDEN = 2816

A_Q = A_HEADS * HEAD_DIM
A_KV = A_KV_HEADS * HEAD_DIM
B_QK = B_HEADS * 2 * HEAD_DIM
B_V = B_HEADS * B_V_DIM
EVEN_IN = A_Q + 2 * A_KV + 2 * B_QK + B_V
C_Q = C_HEADS * HEAD_DIM
C_KV = C_KV_HEADS * HEAD_DIM
ODD_IN = C_Q + 2 * C_KV

LOG2E = 1.4426950408889634
Q_SCALE = HEAD_DIM ** -0.5 * LOG2E

TOKEN_TILE = 512
Q_TILE = 256
LANE_TILE = 256
KEY_CHUNK = 1024
FFN_CHUNK = 256
VMEM_LIMIT = 56 * 1024 * 1024
K_LANES = 128
SUM_ROWS = 16
REF_ROW = HEAD_DIM
OVERFLOW_GUARD = 100.0

_TAB_ROWS = 80


def _rope_table(pos, dim, theta):
    inv = theta ** (-jnp.arange(0, dim, 2, dtype=F32) / dim)
    ang = pos.astype(F32)[:, None] * inv[None, :]
    return jnp.cos(ang), jnp.sin(ang)


def _position_table(seq):
    rows = seq // GRID_W
    t_row = jnp.broadcast_to(jnp.arange(rows)[:, None], (rows, GRID_W)).reshape(-1)
    t_col = jnp.broadcast_to(jnp.arange(GRID_W)[None, :], (rows, GRID_W)).reshape(-1)
    row_cos, row_sin = _rope_table(t_row, HEAD_DIM // 2, AXIAL_THETA)
    col_cos, col_sin = _rope_table(t_col, HEAD_DIM // 2, AXIAL_THETA)
    cos, sin = _rope_table(jnp.arange(seq), ROPE_DIM, ROPE_THETA)
    return jnp.concatenate([row_cos, row_sin, col_cos, col_sin, cos, sin], axis=1).T


def _rms(x, g):
    return x * lax.rsqrt(jnp.mean(x * x, axis=-1, keepdims=True) + EPS) * g


def _axial_rope_t(t, tab):
    rc, rs, cc, cs = tab[0:16][None], tab[16:32][None], tab[32:48][None], tab[48:64][None]
    x1, x2, x3, x4 = t[:, 0:16], t[:, 16:32], t[:, 32:48], t[:, 48:64]
    return jnp.concatenate(
        [x1 * rc - x2 * rs, x2 * rc + x1 * rs, x3 * cc - x4 * cs, x4 * cc + x3 * cs], axis=1)


def _partial_rope_t(t, tab):
    c, s = tab[64:72][None], tab[72:80][None]
    x1, x2 = t[:, 0:8], t[:, 8:16]
    return jnp.concatenate([x1 * c - x2 * s, x2 * c + x1 * s, t[:, 16:]], axis=1)


def _head_rms_t(t, g):
    r = lax.rsqrt(jnp.mean(t * t, axis=1, keepdims=True) + EPS)
    return t * r * g[None]


def _unit_row_block(heads, rows, n):
    return (lax.broadcasted_iota(jnp.int32, (heads, rows, n), 1) == 0).astype(F32)


def _with_ones_column(t):
    heads, d, n = t.shape
    aug = jnp.concatenate([t, _unit_row_block(heads, K_LANES - d, n)], axis=1)
    return aug.reshape(heads * K_LANES, n)


def _with_ones_rows(t):
    heads, dv, n = t.shape
    aug = jnp.concatenate([t, _unit_row_block(heads, SUM_ROWS, n)], axis=1)
    return aug.reshape(heads * (dv + SUM_ROWS), n)


def _proj_t(wt_ref, lo, hi, h):
    return lax.dot_general(wt_ref[lo:hi, :], h, (((1,), (1,)), ((), ())),
                           preferred_element_type=F32)


def _even_in_kernel(x_ref, g_ref, wt_ref, tab_ref, qkg_ref,
                    qa_ref, ka_ref, va_ref, qb_ref, kb_ref, vb_ref):
    n = x_ref.shape[0]
    h = _rms(x_ref[...], g_ref[...]).astype(BF16)
    tab = tab_ref[...]
    o = 0
    t = _proj_t(wt_ref, o, o + A_Q, h).reshape(A_HEADS, HEAD_DIM, n)
    t = _axial_rope_t(_head_rms_t(t, qkg_ref[0]), tab) * Q_SCALE
    qa_ref[...] = t.reshape(A_Q, n).astype(BF16)
    o += A_Q
    t = _proj_t(wt_ref, o, o + A_KV, h).reshape(A_KV_HEADS, HEAD_DIM, n)
    t = _axial_rope_t(_head_rms_t(t, qkg_ref[1]), tab)
    ka_ref[...] = _with_ones_column(t).T.astype(BF16)
    o += A_KV
    t = _proj_t(wt_ref, o, o + A_KV, h).reshape(A_KV_HEADS, HEAD_DIM, n)
    va_ref[...] = _with_ones_rows(t).astype(BF16)
    o += A_KV
    t = _proj_t(wt_ref, o, o + B_QK, h).reshape(2 * B_HEADS, HEAD_DIM, n)
    qb_ref[...] = (_partial_rope_t(t, tab) * Q_SCALE).reshape(B_QK, n).astype(BF16)
    o += B_QK
    t = _proj_t(wt_ref, o, o + B_QK, h).reshape(2 * B_HEADS, HEAD_DIM, n)
    kb_ref[...] = _with_ones_column(_partial_rope_t(t, tab)).T.astype(BF16)
    o += B_QK
    t = _proj_t(wt_ref, o, o + B_V, h).reshape(B_HEADS, B_V_DIM, n)
    vb_ref[...] = _with_ones_rows(t).astype(BF16)


def _odd_in_kernel(x_ref, g_ref, wt_ref, tab_ref, q_ref, k_ref, v_ref):
    n = x_ref.shape[0]
    h = _rms(x_ref[...], g_ref[...]).astype(BF16)
    tab = tab_ref[...]
    t = _proj_t(wt_ref, 0, C_Q, h).reshape(C_HEADS, HEAD_DIM, n)
    q_ref[...] = (_partial_rope_t(t, tab) * Q_SCALE).reshape(C_Q, n).astype(BF16)
    t = _proj_t(wt_ref, C_Q, C_Q + C_KV, h).reshape(C_KV_HEADS, HEAD_DIM, n)
    k_ref[...] = _partial_rope_t(t, tab).reshape(C_KV, n).T.astype(BF16)
    v_ref[...] = _proj_t(wt_ref, C_Q + C_KV, ODD_IN, h).astype(BF16)


def _resident(shape):
    nd = len(shape)
    return pl.BlockSpec(shape, lambda *_: (0,) * nd, pipeline_mode=pl.Buffered(1))


def _token_params():
    return pltpu.CompilerParams(dimension_semantics=("parallel",), vmem_limit_bytes=VMEM_LIMIT)


def _input_projection(x, gain, wt, tab, pos_tile, qk_gain=None):
    tokens = x.shape[0]
    tm = TOKEN_TILE
    grid = (tokens // tm,)
    in_specs = [
        pl.BlockSpec((tm, D_MODEL), lambda i: (i, 0)),
        _resident((1, D_MODEL)),
        _resident(wt.shape),
        pl.BlockSpec((_TAB_ROWS, tm), lambda i: (0, pos_tile(i))),
    ]
    args = [x, gain.reshape(1, D_MODEL), wt, tab]

    def feat(rows):
        return (jax.ShapeDtypeStruct((rows, tokens), BF16), pl.BlockSpec((rows, tm), lambda i: (0, i)))

    def tokm(cols):
        return (jax.ShapeDtypeStruct((tokens, cols), BF16), pl.BlockSpec((tm, cols), lambda i: (i, 0)))

    if qk_gain is not None:
        in_specs.append(_resident((2, HEAD_DIM, 1)))
        args.append(qk_gain.reshape(2, HEAD_DIM, 1))
        outs = [feat(A_Q), tokm(A_KV_HEADS * K_LANES), feat(A_KV_HEADS * (HEAD_DIM + SUM_ROWS)),
                feat(B_QK), tokm(2 * B_HEADS * K_LANES), feat(B_HEADS * (B_V_DIM + SUM_ROWS))]
        body, name = _even_in_kernel, "even_in_proj"
    else:
        outs = [feat(C_Q), tokm(C_KV), feat(C_KV)]
        body, name = _odd_in_kernel, "odd_in_proj"
    return pl.pallas_call(
        body, grid=grid, in_specs=in_specs,
        out_specs=[o[1] for o in outs], out_shape=[o[0] for o in outs],
        compiler_params=_token_params(), name=name)(*args)


def _flash_t(k_refs, vt_ref, qaug_ref, s_a, s_b, acc_ref, key_chunk):
    n = qaug_ref.shape[1]
    width = n // len(k_refs)
    nc = k_refs[0].shape[0] // key_chunk
    assert nc >= 2 and nc % 2 == 0
    ref_rows = lax.broadcasted_iota(jnp.int32, (SUM_ROWS, n), 0) == 0

    def bf16_round(x):
        return x.astype(BF16).astype(F32)

    def set_reference(ref):
        qaug_ref[REF_ROW:REF_ROW + SUM_ROWS, :] = jnp.where(ref_rows, -ref, 0.0).astype(BF16)

    def tile_scores(t, off, rows, s_ref):
        lanes = slice(t * LANE_TILE, (t + 1) * LANE_TILE)
        k_ref = k_refs[t * LANE_TILE // width]
        s = jnp.dot(k_ref[pl.ds(off, rows), :], qaug_ref[:, lanes], preferred_element_type=F32)
        if s_ref is not None:
            s_ref[:, lanes] = s.astype(BF16)
        return jnp.max(s, axis=0, keepdims=True)

    def tile_accumulate(t, off, s_ref, alpha):
        lanes = slice(t * LANE_TILE, (t + 1) * LANE_TILE)
        p = jnp.exp2(s_ref[:, lanes])
        pv = jnp.dot(vt_ref[:, pl.ds(off, key_chunk)], p, preferred_element_type=F32)
        acc_ref[:, lanes] = alpha[:, lanes] * acc_ref[:, lanes] + pv

    def step(c_next, s_next, c_cur, s_cur, alpha):
        maxes = []
        for t in range(n // LANE_TILE):
            if c_next is not None:
                maxes.append(tile_scores(t, pl.multiple_of(c_next * key_chunk, key_chunk), key_chunk, s_next))
            if c_cur is not None:
                tile_accumulate(t, pl.multiple_of(c_cur * key_chunk, key_chunk), s_cur, alpha)
        return jnp.concatenate(maxes, axis=1) if maxes else None

    def sweep(top0):
        acc_ref[...] = jnp.zeros(acc_ref.shape, F32)
        ref0 = bf16_round(top0)
        set_reference(ref0)
        mc_a = step(0, s_a, None, None, None)

        def body(i, carry):
            ref_prev, ref_a, mc_a, top, over = carry
            ref_b = bf16_round(top)
            set_reference(ref_b)
            mc_b = step(2 * i + 1, s_b, 2 * i, s_a, jnp.exp2(ref_prev - ref_a))
            top = jnp.maximum(top, ref_a + mc_a)
            ref_a2 = bf16_round(top)
            set_reference(ref_a2)
            mc_a2 = step(2 * i + 2, s_a, 2 * i + 1, s_b, jnp.exp2(ref_a - ref_b))
            top = jnp.maximum(top, ref_b + mc_b)
            return ref_b, ref_a2, mc_a2, top, jnp.maximum(over, jnp.maximum(mc_a, mc_b))

        ref_prev, ref_a, mc_a, top, over = lax.fori_loop(
            0, nc // 2 - 1, body, (ref0, ref0, mc_a, top0, jnp.zeros((1, n), F32)))
        ref_b = bf16_round(top)
        set_reference(ref_b)
        mc_b = step(nc - 1, s_b, nc - 2, s_a, jnp.exp2(ref_prev - ref_a))
        step(None, None, nc - 1, s_b, jnp.exp2(ref_a - ref_b))
        top = jnp.maximum(top, jnp.maximum(ref_a + mc_a, ref_b + mc_b))
        return top, jnp.maximum(over, jnp.maximum(mc_a, mc_b))

    set_reference(jnp.zeros((1, n), F32))
    first = jnp.concatenate([tile_scores(t, 0, K_LANES, None) for t in range(n // LANE_TILE)], axis=1)
    top, over = sweep(first)

    @pl.when(jnp.max(over) > OVERFLOW_GUARD)
    def _():
        sweep(top)


def _finish(acc_ref, dv):
    acc = acc_ref[...]
    return acc[:dv] / acc[dv:dv + 1]


def _heads_to_token_major(o, heads, tq):
    stacked = jnp.concatenate([o[:, g * tq:(g + 1) * tq] for g in range(heads)], axis=0)
    return stacked.T


def _fill_queries(qaug_ref, q_ref, heads, tq):
    n = qaug_ref.shape[1]
    for g in range(heads):
        qaug_ref[0:HEAD_DIM, g * tq:(g + 1) * tq] = q_ref[g * HEAD_DIM:(g + 1) * HEAD_DIM, :]
    qaug_ref[REF_ROW + SUM_ROWS:, :] = jnp.zeros((K_LANES - REF_ROW - SUM_ROWS, n), BF16)


def _attn_a_kernel(q_ref, k_ref, vt_ref, o_ref, qaug_ref, s_a, s_b, acc_ref, *, key_chunk):
    tq = q_ref.shape[1]
    group = A_HEADS // A_KV_HEADS
    _fill_queries(qaug_ref, q_ref, group, tq)
    _flash_t([k_ref], vt_ref, qaug_ref, s_a, s_b, acc_ref, key_chunk)
    o_ref[...] = _heads_to_token_major(_finish(acc_ref, HEAD_DIM), group, tq).astype(BF16)


def _attn_b_kernel(q_ref, k0_ref, k1_ref, vt_ref, lam_ref, o_ref, qaug_ref, s_a, s_b, acc_ref, *,
                   key_chunk, lam_init):
    tq = q_ref.shape[1]
    _fill_queries(qaug_ref, q_ref, 2, tq)
    _flash_t([k0_ref, k1_ref], vt_ref, qaug_ref, s_a, s_b, acc_ref, key_chunk)
    lf = lam_ref[...]
    lam = (jnp.exp(jnp.sum(lf[0:1] * lf[1:2], axis=1, keepdims=True))
           - jnp.exp(jnp.sum(lf[2:3] * lf[3:4], axis=1, keepdims=True)) + lam_init)
    o = _finish(acc_ref, B_V_DIM)
    o = o[:, :tq] - lam * o[:, tq:]
    o = o * lax.rsqrt(jnp.mean(o * o, axis=0, keepdims=True) + EPS) * (1.0 - lam_init)
    o_ref[...] = o.T.astype(BF16)


def _select_rows(q, idx, count):
    zero = jnp.zeros_like(q)
    return jnp.concatenate([jnp.where(idx == r, q, zero) for r in range(count)], axis=0)


def _attn_c_kernel(sink_ref, q_ref, k_ref, vt_ref, o_ref):
    tq = q_ref.shape[1]
    seq = k_ref.shape[0]
    group = C_HEADS // C_KV_HEADS
    span = tq + 2 * WINDOW
    j = pl.program_id(1)
    q0 = pl.program_id(2) * tq
    start = pl.multiple_of(jnp.clip(q0 - WINDOW, 0, seq - span), WINDOW)
    keys = k_ref[pl.ds(start, span), :]
    values = vt_ref[:, pl.ds(start, span)]
    rel = (lax.broadcasted_iota(jnp.int32, (span, tq), 0)
           - lax.broadcasted_iota(jnp.int32, (span, tq), 1)) + (start - q0)
    valid = jnp.abs(rel) <= WINDOW
    outs = []
    for g in range(group):
        qext = _select_rows(q_ref[g * HEAD_DIM:(g + 1) * HEAD_DIM, :], j, C_KV_HEADS)
        s = jnp.dot(keys, qext, preferred_element_type=F32)
        s = jnp.where(valid, s, -jnp.inf)
        sink = sink_ref[j * group + g] * LOG2E
        m = jnp.maximum(jnp.max(s, axis=0, keepdims=True), sink)
        p = jnp.exp2(s - m)
        l = jnp.exp2(sink - m) + jnp.sum(p, axis=0, keepdims=True)
        outs.append(jnp.dot(values, p.astype(BF16), preferred_element_type=F32) / l)
    o_ref[...] = jnp.concatenate(outs, axis=0).T.astype(BF16)


def _attn_params():
    return pltpu.CompilerParams(dimension_semantics=("parallel", "parallel", "arbitrary"),
                                vmem_limit_bytes=VMEM_LIMIT)


def _flash_scratch(key_chunk, n, dv):
    return [pltpu.VMEM((K_LANES, n), BF16), pltpu.VMEM((key_chunk, n), BF16),
            pltpu.VMEM((key_chunk, n), BF16), pltpu.VMEM((dv + SUM_ROWS, n), F32)]


def _attention_a(qt, k, vt, batch, seq):
    tq = 2 * Q_TILE
    nq = seq // tq
    group = A_HEADS // A_KV_HEADS
    width = group * HEAD_DIM
    key_chunk = min(KEY_CHUNK, seq // 2)
    return pl.pallas_call(
        functools.partial(_attn_a_kernel, key_chunk=key_chunk),
        grid=(batch, A_KV_HEADS, nq),
        scratch_shapes=_flash_scratch(key_chunk, group * tq, HEAD_DIM),
        in_specs=[
            pl.BlockSpec((width, tq), lambda b, j, i: (j, b * nq + i)),
            pl.BlockSpec((seq, K_LANES), lambda b, j, i: (b, j)),
            pl.BlockSpec((HEAD_DIM + SUM_ROWS, seq), lambda b, j, i: (j, b)),
        ],
        out_specs=pl.BlockSpec((tq, width), lambda b, j, i: (b * nq + i, j)),
        out_shape=jax.ShapeDtypeStruct((batch * seq, A_Q), BF16),
        compiler_params=_attn_params(), name="attn_axial")(qt, k, vt)


def _attention_b(qt, k, vt, lam_w, lam_init, batch, seq):
    tq = 4 * Q_TILE
    nq = seq // tq
    key_chunk = min(KEY_CHUNK, seq // 2)
    return pl.pallas_call(
        functools.partial(_attn_b_kernel, key_chunk=key_chunk, lam_init=lam_init),
        grid=(batch, B_HEADS, nq),
        scratch_shapes=_flash_scratch(key_chunk, 2 * tq, B_V_DIM),
        in_specs=[
            pl.BlockSpec((2 * HEAD_DIM, tq), lambda b, j, i: (j, b * nq + i)),
            pl.BlockSpec((seq, K_LANES), lambda b, j, i: (b, 2 * j)),
            pl.BlockSpec((seq, K_LANES), lambda b, j, i: (b, 2 * j + 1)),
            pl.BlockSpec((B_V_DIM + SUM_ROWS, seq), lambda b, j, i: (j, b)),
            pl.BlockSpec((4, HEAD_DIM), lambda b, j, i: (0, 0)),
        ],
        out_specs=pl.BlockSpec((tq, B_V_DIM), lambda b, j, i: (b * nq + i, j)),
        out_shape=jax.ShapeDtypeStruct((batch * seq, B_V), BF16),
        compiler_params=_attn_params(), name="attn_diff")(qt, k, k, vt, lam_w)


def _attention_c(qt, k, vt, sink, batch, seq):
    tq = Q_TILE
    nq = seq // tq
    width = (C_HEADS // C_KV_HEADS) * HEAD_DIM
    return pl.pallas_call(
        _attn_c_kernel,
        grid=(batch, C_KV_HEADS, nq),
        in_specs=[
            pl.BlockSpec(memory_space=pltpu.SMEM),
            pl.BlockSpec((width, tq), lambda b, j, i: (j, b * nq + i)),
            pl.BlockSpec((seq, C_KV), lambda b, j, i: (b, 0)),
            pl.BlockSpec((HEAD_DIM, seq), lambda b, j, i: (j, b)),
        ],
        out_specs=pl.BlockSpec((tq, width), lambda b, j, i: (b * nq + i, j)),
        out_shape=jax.ShapeDtypeStruct((batch * seq, C_Q), BF16),
        compiler_params=_attn_params(), name="attn_window")(sink, qt, k, vt)


def _post_kernel(lo_ref, hi_ref, x_ref, wo_ref, gmp_ref, gfp_ref, wgu_ref, wd_ref, gfo_ref, y_ref):
    half = lo_ref.shape[1]
    mix = (jnp.dot(lo_ref[...], wo_ref[0:half, :], preferred_element_type=F32)
           + jnp.dot(hi_ref[...], wo_ref[half:, :], preferred_element_type=F32))
    x = x_ref[...] + _rms(mix, gmp_ref[...])
    h = _rms(x, gfp_ref[...]).astype(BF16)
    acc = jnp.zeros(x.shape, F32)
    for c in range(FFN_HIDDEN // FFN_CHUNK):
        lo = c * FFN_CHUNK
        g = jnp.dot(h, wgu_ref[:, lo:lo + FFN_CHUNK], preferred_element_type=F32)
        u = jnp.dot(h, wgu_ref[:, FFN_HIDDEN + lo:FFN_HIDDEN + lo + FFN_CHUNK], preferred_element_type=F32)
        a = (g * (1.0 / (1.0 + jnp.exp(-g))) * u).astype(BF16)
        acc = acc + jnp.dot(a, wd_ref[lo:lo + FFN_CHUNK, :], preferred_element_type=F32)
    y_ref[...] = x + _rms(acc, gfo_ref[...])


def _post_attention(o_lo, o_hi, lo_blk, hi_blk, x, w_out, g_mix_post, g_ffn_pre, w_gu, w_down, g_ffn_post):
    tokens = x.shape[0]
    tm = TOKEN_TILE
    half = D_MODEL // 2
    row = lambda v: v.reshape(1, D_MODEL)
    return pl.pallas_call(
        _post_kernel, grid=(tokens // tm,),
        in_specs=[
            pl.BlockSpec((tm, half), lambda i: (i, lo_blk)),
            pl.BlockSpec((tm, half), lambda i: (i, hi_blk)),
            pl.BlockSpec((tm, D_MODEL), lambda i: (i, 0)),
            _resident(w_out.shape), _resident((1, D_MODEL)), _resident((1, D_MODEL)),
            _resident(w_gu.shape), _resident(w_down.shape), _resident((1, D_MODEL)),
        ],
        out_specs=pl.BlockSpec((tm, D_MODEL), lambda i: (i, 0)),
        out_shape=jax.ShapeDtypeStruct((tokens, D_MODEL), F32),
        compiler_params=_token_params(), name="out_proj_ffn")(
            o_lo, o_hi, x, w_out, row(g_mix_post), row(g_ffn_pre), w_gu, w_down, row(g_ffn_post))


def _trunk(x3, weights, tab):
    batch, seq, _ = x3.shape
    assert seq % TOKEN_TILE == 0 and seq % (4 * Q_TILE) == 0 and seq >= Q_TILE + 2 * WINDOW
    x = x3.reshape(batch * seq, D_MODEL)
    tiles_per_seq = seq // TOKEN_TILE
    pos_tile = lambda i: i % tiles_per_seq
    for l, w in enumerate(weights):
        if l % 2 == 0:
            qa, ka, va, qb, kb, vb = _input_projection(x, w["g_mix_pre"], w["wt_in"], tab, pos_tile, w["qk_gain"])
            o_lo = _attention_a(qa, ka, va, batch, seq)
            o_hi = _attention_b(qb, kb, vb, w["lam_w"], w["lam_init"], batch, seq)
            lo_blk, hi_blk = 0, 0
        else:
            q, k, v = _input_projection(x, w["g_mix_pre"], w["wt_in"], tab, pos_tile)
            o_lo = o_hi = _attention_c(q, k, v, w["sink"], batch, seq)
            lo_blk, hi_blk = 0, 1
        x = _post_attention(o_lo, o_hi, lo_blk, hi_blk, x, w["w_out"], w["g_mix_post"], w["g_ffn_pre"],
                            w["w_gu"], w["w_down"], w["g_ffn_post"])
    return x.reshape(batch, seq, D_MODEL)


def kernel(x_prompt, x_sample, w_in_even, w_out_even, qk_norm_a, diff_lambda, w_in_odd, w_out_odd,
           sink_c, w_gate_up, w_down, norm_mix_pre, norm_mix_post, norm_ffn_pre, norm_ffn_post):
    weights = []
    for l in range(w_gate_up.shape[0]):
        e = l // 2
        w = dict(g_mix_pre=norm_mix_pre[l], g_mix_post=norm_mix_post[l], g_ffn_pre=norm_ffn_pre[l],
                 g_ffn_post=norm_ffn_post[l], w_gu=w_gate_up[l].astype(BF16), w_down=w_down[l].astype(BF16))
        if l % 2 == 0:
            w.update(wt_in=w_in_even[e].T.astype(BF16), w_out=w_out_even[e].astype(BF16),
                     qk_gain=qk_norm_a[e], lam_w=diff_lambda[e], lam_init=0.8 - 0.6 * math.exp(-0.3 * l))
        else:
            w.update(wt_in=w_in_odd[e].T.astype(BF16), w_out=w_out_odd[e].astype(BF16), sink=sink_c[e])
        weights.append(w)
    tab = _position_table(max(x_prompt.shape[1], x_sample.shape[1]))
    return (_trunk(x_prompt, weights, tab), _trunk(x_sample, weights, tab))
```

```python
import functools
import math

import jax
import jax.numpy as jnp
from jax import lax
from jax.experimental import pallas as pl
from jax.experimental.pallas import tpu as pltpu

F32 = jnp.float32
BF16 = jnp.bfloat16

D_MODEL = 1024
HEAD_DIM = 64
GRID_W = 64
EPS = 1e-6
A_HEADS = 8
A_KV_HEADS = 2
AXIAL_THETA = 10000.0
B_HEADS = 4
B_V_DIM = 2 * HEAD_DIM
C_HEADS = 16
C_KV_HEADS = 4
WINDOW = 128
ROPE_THETA = 500000.0
ROPE_DIM = HEAD_DIM // 4
FFN_HIDDEN = 2816

A_Q = A_HEADS * HEAD_DIM
A_KV = A_KV_HEADS * HEAD_DIM
B_QK = B_HEADS * 2 * HEAD_DIM
B_V = B_HEADS * B_V_DIM
EVEN_IN = A_Q + 2 * A_KV + 2 * B_QK + B_V
C_Q = C_HEADS * HEAD_DIM
C_KV = C_KV_HEADS * HEAD_DIM
ODD_IN = C_Q + 2 * C_KV

LOG2E = 1.4426950408889634
Q_SCALE = HEAD_DIM ** -0.5 * LOG2E

TOKEN_TILE = 512
Q_TILE = 256
LANE_TILE = 256
WINDOW_TILES = 4
KEY_CHUNK = 512
STEPS_PER_ITER = 4
FFN_CHUNK = 256
VMEM_LIMIT = 56 * 1024 * 1024
K_LANES = 128
SUM_ROWS = 16
REF_ROW = HEAD_DIM
OVERFLOW_GUARD = 100.0

_TAB_ROWS = 80


def _rope_table(pos, dim, theta):
    inv = theta ** (-jnp.arange(0, dim, 2, dtype=F32) / dim)
    ang = pos.astype(F32)[:, None] * inv[None, :]
    return jnp.cos(ang), jnp.sin(ang)


def _position_table(seq):
    rows = seq // GRID_W
    t_row = jnp.broadcast_to(jnp.arange(rows)[:, None], (rows, GRID_W)).reshape(-1)
    t_col = jnp.broadcast_to(jnp.arange(GRID_W)[None, :], (rows, GRID_W)).reshape(-1)
    row_cos, row_sin = _rope_table(t_row, HEAD_DIM // 2, AXIAL_THETA)
    col_cos, col_sin = _rope_table(t_col, HEAD_DIM // 2, AXIAL_THETA)
    cos, sin = _rope_table(jnp.arange(seq), ROPE_DIM, ROPE_THETA)
    return jnp.concatenate([row_cos, row_sin, col_cos, col_sin, cos, sin], axis=1).T


def _rms(x, g):
    return x * lax.rsqrt(jnp.mean(x * x, axis=-1, keepdims=True) + EPS) * g


def _axial_rope_t(t, tab):
    rc, rs, cc, cs = tab[0:16][None], tab[16:32][None], tab[32:48][None], tab[48:64][None]
    x1, x2, x3, x4 = t[:, 0:16], t[:, 16:32], t[:, 32:48], t[:, 48:64]
    return jnp.concatenate(
        [x1 * rc - x2 * rs, x2 * rc + x1 * rs, x3 * cc - x4 * cs, x4 * cc + x3 * cs], axis=1)


def _partial_rope_t(t, tab):
    c, s = tab[64:72][None], tab[72:80][None]
    x1, x2 = t[:, 0:8], t[:, 8:16]
    return jnp.concatenate([x1 * c - x2 * s, x2 * c + x1 * s, t[:, 16:]], axis=1)


def _head_rms_t(t, g):
    r = lax.rsqrt(jnp.mean(t * t, axis=1, keepdims=True) + EPS)
    return t * r * g[None]


def _unit_row_block(heads, rows, n):
    return (lax.broadcasted_iota(jnp.int32, (heads, rows, n), 1) == 0).astype(F32)


def _with_ones_column(t):
    heads, d, n = t.shape
    aug = jnp.concatenate([t, _unit_row_block(heads, K_LANES - d, n)], axis=1)
    return aug.reshape(heads * K_LANES, n)


def _with_ones_rows(t):
    heads, dv, n = t.shape
    aug = jnp.concatenate([t, _unit_row_block(heads, SUM_ROWS, n)], axis=1)
    return aug.reshape(heads * (dv + SUM_ROWS), n)


def _proj_t(wt_ref, lo, hi, h):
    return lax.dot_general(wt_ref[lo:hi, :], h, (((1,), (1,)), ((), ())),
                           preferred_element_type=F32)


def _even_in_kernel(x_ref, g_ref, wt_ref, tab_ref, qkg_ref,
                    qa_ref, ka_ref, va_ref, qb_ref, kb_ref, vb_ref):
    n = x_ref.shape[0]
    h = _rms(x_ref[...], g_ref[...]).astype(BF16)
    tab = tab_ref[...]
    o = 0
    t = _proj_t(wt_ref, o, o + A_Q, h).reshape(A_HEADS, HEAD_DIM, n)
    t = _axial_rope_t(_head_rms_t(t, qkg_ref[0]), tab) * Q_SCALE
    qa_ref[...] = t.reshape(A_Q, n).astype(BF16)
    o += A_Q
    t = _proj_t(wt_ref, o, o + A_KV, h).reshape(A_KV_HEADS, HEAD_DIM, n)
    t = _axial_rope_t(_head_rms_t(t, qkg_ref[1]), tab)
    ka_ref[...] = _with_ones_column(t).T.astype(BF16)
    o += A_KV
    t = _proj_t(wt_ref, o, o + A_KV, h).reshape(A_KV_HEADS, HEAD_DIM, n)
    va_ref[...] = _with_ones_rows(t).astype(BF16)
    o += A_KV
    t = _proj_t(wt_ref, o, o + B_QK, h).reshape(2 * B_HEADS, HEAD_DIM, n)
    qb_ref[...] = (_partial_rope_t(t, tab) * Q_SCALE).reshape(B_QK, n).astype(BF16)
    o += B_QK
    t = _proj_t(wt_ref, o, o + B_QK, h).reshape(2 * B_HEADS, HEAD_DIM, n)
    kb_ref[...] = _with_ones_column(_partial_rope_t(t, tab)).T.astype(BF16)
    o += B_QK
    t = _proj_t(wt_ref, o, o + B_V, h).reshape(B_HEADS, B_V_DIM, n)
    vb_ref[...] = _with_ones_rows(t).astype(BF16)


def _odd_in_kernel(x_ref, g_ref, wt_ref, tab_ref, q_ref, k_ref, v_ref):
    n = x_ref.shape[0]
    h = _rms(x_ref[...], g_ref[...]).astype(BF16)
    tab = tab_ref[...]
    t = _proj_t(wt_ref, 0, C_Q, h).reshape(C_HEADS, HEAD_DIM, n)
    q_ref[...] = (_partial_rope_t(t, tab) * Q_SCALE).reshape(C_Q, n).astype(BF16)
    t = _proj_t(wt_ref, C_Q, C_Q + C_KV, h).reshape(C_KV_HEADS, HEAD_DIM, n)
    k_ref[...] = _with_ones_column(_partial_rope_t(t, tab)).T.astype(BF16)
    t = _proj_t(wt_ref, C_Q + C_KV, ODD_IN, h).reshape(C_KV_HEADS, HEAD_DIM, n)
    v_ref[...] = _with_ones_rows(t).astype(BF16)


def _resident(shape):
    nd = len(shape)
    return pl.BlockSpec(shape, lambda *_: (0,) * nd, pipeline_mode=pl.Buffered(1))


def _token_params():
    return pltpu.CompilerParams(dimension_semantics=("parallel",), vmem_limit_bytes=VMEM_LIMIT)


def _input_projection(x, gain, wt, tab, pos_tile, qk_gain=None):
    tokens = x.shape[0]
    tm = TOKEN_TILE
    grid = (tokens // tm,)
    in_specs = [
        pl.BlockSpec((tm, D_MODEL), lambda i: (i, 0)),
        _resident((1, D_MODEL)),
        _resident(wt.shape),
        pl.BlockSpec((_TAB_ROWS, tm), lambda i: (0, pos_tile(i))),
    ]
    args = [x, gain.reshape(1, D_MODEL), wt, tab]

    def feat(rows):
        return (jax.ShapeDtypeStruct((rows, tokens), BF16), pl.BlockSpec((rows, tm), lambda i: (0, i)))

    def tokm(cols):
        return (jax.ShapeDtypeStruct((tokens, cols), BF16), pl.BlockSpec((tm, cols), lambda i: (i, 0)))

    if qk_gain is not None:
        in_specs.append(_resident((2, HEAD_DIM, 1)))
        args.append(qk_gain.reshape(2, HEAD_DIM, 1))
        outs = [feat(A_Q), tokm(A_KV_HEADS * K_LANES), feat(A_KV_HEADS * (HEAD_DIM + SUM_ROWS)),
                feat(B_QK), tokm(2 * B_HEADS * K_LANES), feat(B_HEADS * (B_V_DIM + SUM_ROWS))]
        body, name = _even_in_kernel, "even_in_proj"
    else:
        outs = [feat(C_Q), tokm(C_KV_HEADS * K_LANES), feat(C_KV_HEADS * (HEAD_DIM + SUM_ROWS))]
        body, name = _odd_in_kernel, "odd_in_proj"
    return pl.pallas_call(
        body, grid=grid, in_specs=in_specs,
        out_specs=[o[1] for o in outs], out_shape=[o[0] for o in outs],
        compiler_params=_token_params(), name=name)(*args)


def _flash_t(k_refs, vt_ref, qaug_ref, s_a, s_b, acc_ref, key_chunk):
    n = qaug_ref.shape[1]
    width = n // len(k_refs)
    nc = k_refs[0].shape[0] // key_chunk
    assert nc >= 2 and STEPS_PER_ITER % 2 == 0
    ref_rows = lax.broadcasted_iota(jnp.int32, (SUM_ROWS, n), 0) == 0

    def bf16_round(x):
        return x.astype(BF16).astype(F32)

    def set_reference(ref):
        qaug_ref[REF_ROW:REF_ROW + SUM_ROWS, :] = jnp.where(ref_rows, -ref, 0.0).astype(BF16)

    def tile_scores(t, off, rows, s_ref):
        lanes = slice(t * LANE_TILE, (t + 1) * LANE_TILE)
        k_ref = k_refs[t * LANE_TILE // width]
        s = jnp.dot(k_ref[pl.ds(off, rows), :], qaug_ref[:, lanes], preferred_element_type=F32)
        if s_ref is not None:
            s_ref[:, lanes] = s.astype(BF16)
        return jnp.max(s, axis=0, keepdims=True)

    def tile_accumulate(t, off, s_ref, alpha):
        lanes = slice(t * LANE_TILE, (t + 1) * LANE_TILE)
        p = jnp.exp2(s_ref[:, lanes])
        pv = jnp.dot(vt_ref[:, pl.ds(off, key_chunk)], p, preferred_element_type=F32)
        acc_ref[:, lanes] = alpha[:, lanes] * acc_ref[:, lanes] + pv

    def step(c_next, s_next, c_cur, s_cur, alpha):
        maxes = []
        for t in range(n // LANE_TILE):
            if c_next is not None:
                maxes.append(tile_scores(t, pl.multiple_of(c_next * key_chunk, key_chunk), key_chunk, s_next))
            if c_cur is not None:
                tile_accumulate(t, pl.multiple_of(c_cur * key_chunk, key_chunk), s_cur, alpha)
        return jnp.concatenate(maxes, axis=1) if maxes else None

    def sweep(top0):
        bufs = (s_a, s_b)
        acc_ref[...] = jnp.zeros(acc_ref.shape, F32)
        ref0 = bf16_round(top0)
        set_reference(ref0)
        mc0 = step(0, s_a, None, None, None)

        def overlapped_step(k, parity, state):
            ref_pp, ref_p, mc_p, top, over = state
            ref_k = bf16_round(top)
            set_reference(ref_k)
            mc_k = step(k, bufs[parity], k - 1, bufs[1 - parity], jnp.exp2(ref_pp - ref_p))
            return ref_p, ref_k, mc_k, jnp.maximum(top, ref_p + mc_p), jnp.maximum(over, mc_p)

        def body(i, state):
            for u in range(STEPS_PER_ITER):
                state = overlapped_step(STEPS_PER_ITER * i + 1 + u, (1 + u) % 2, state)
            return state

        state = (ref0, ref0, mc0, top0, jnp.zeros((1, n), F32))
        iters = (nc - 1) // STEPS_PER_ITER
        state = lax.fori_loop(0, iters, body, state)
        for k in range(iters * STEPS_PER_ITER + 1, nc):
            state = overlapped_step(k, k % 2, state)
        ref_pp, ref_p, mc_p, top, over = state
        step(None, None, nc - 1, bufs[(nc - 1) % 2], jnp.exp2(ref_pp - ref_p))
        return jnp.maximum(top, ref_p + mc_p), jnp.maximum(over, mc_p)

    set_reference(jnp.zeros((1, n), F32))
    first = jnp.concatenate([tile_scores(t, 0, K_LANES, None) for t in range(n // LANE_TILE)], axis=1)
    top, over = sweep(first)

    @pl.when(jnp.max(over) > OVERFLOW_GUARD)
    def _():
        sweep(top)


def _finish(acc_ref, dv):
    acc = acc_ref[...]
    return acc[:dv] / acc[dv:dv + 1]


def _heads_to_token_major(o, heads, tq):
    stacked = jnp.concatenate([o[:, g * tq:(g + 1) * tq] for g in range(heads)], axis=0)
    return stacked.T


def _fill_queries(qaug_ref, q_ref, heads, tq):
    n = qaug_ref.shape[1]
    for g in range(heads):
        qaug_ref[0:HEAD_DIM, g * tq:(g + 1) * tq] = q_ref[g * HEAD_DIM:(g + 1) * HEAD_DIM, :]
    qaug_ref[REF_ROW + SUM_ROWS:, :] = jnp.zeros((K_LANES - REF_ROW - SUM_ROWS, n), BF16)


def _attn_a_kernel(q_ref, k_ref, vt_ref, o_ref, qaug_ref, s_a, s_b, acc_ref, *, key_chunk):
    tq = q_ref.shape[1]
    group = A_HEADS // A_KV_HEADS
    _fill_queries(qaug_ref, q_ref, group, tq)
    _flash_t([k_ref], vt_ref, qaug_ref, s_a, s_b, acc_ref, key_chunk)
    o_ref[...] = _heads_to_token_major(_finish(acc_ref, HEAD_DIM), group, tq).astype(BF16)


def _attn_b_kernel(q_ref, k0_ref, k1_ref, vt_ref, lam_ref, o_ref, qaug_ref, s_a, s_b, acc_ref, *,
                   key_chunk, lam_init):
    tq = q_ref.shape[1]
    _fill_queries(qaug_ref, q_ref, 2, tq)
    _flash_t([k0_ref, k1_ref], vt_ref, qaug_ref, s_a, s_b, acc_ref, key_chunk)
    lf = lam_ref[...]
    lam = (jnp.exp(jnp.sum(lf[0:1] * lf[1:2], axis=1, keepdims=True))
           - jnp.exp(jnp.sum(lf[2:3] * lf[3:4], axis=1, keepdims=True)) + lam_init)
    o = _finish(acc_ref, B_V_DIM)
    o = o[:, :tq] - lam * o[:, tq:]
    o = o * lax.rsqrt(jnp.mean(o * o, axis=0, keepdims=True) + EPS) * (1.0 - lam_init)
    o_ref[...] = o.T.astype(BF16)


def _attn_c_kernel(sink_ref, q_ref, k_ref, vt_ref, o_ref, qaug_a, qaug_b, s_a, s_b, bias_ref):
    group = C_HEADS // C_KV_HEADS
    seq = k_ref.shape[0]
    tile = q_ref.shape[1] // WINDOW_TILES
    span = tile + 2 * WINDOW
    j = pl.program_id(1)
    q_base = pl.program_id(2) * q_ref.shape[1]
    rel0 = (lax.broadcasted_iota(jnp.int32, (span, tile), 0)
            - lax.broadcasted_iota(jnp.int32, (span, tile), 1))
    sinks = [jnp.full((1, tile), sink_ref[j * group + g] * LOG2E, F32) for g in range(group)]
    for qaug in (qaug_a, qaug_b):
        qaug[HEAD_DIM:, :] = jnp.zeros((K_LANES - HEAD_DIM, group * tile), BF16)

    def window(t):
        q0 = q_base + t * tile
        return q0, pl.multiple_of(jnp.clip(q0 - WINDOW, 0, seq - span), WINDOW)

    def tile_scores(t, g, qaug, s_ref):
        _, start = window(t)
        lanes = slice(g * tile, (g + 1) * tile)
        qaug[0:HEAD_DIM, lanes] = q_ref[g * HEAD_DIM:(g + 1) * HEAD_DIM, t * tile:(t + 1) * tile]
        s = jnp.dot(k_ref[pl.ds(start, span), :], qaug[:, lanes], preferred_element_type=F32)
        s = s + bias_ref[...]
        s_ref[:, lanes] = s
        return jnp.maximum(jnp.max(s, axis=0, keepdims=True), sinks[g])

    def tile_finish(t, g, s_ref, m):
        _, start = window(t)
        lanes = slice(g * tile, (g + 1) * tile)
        p = jnp.exp2(s_ref[:, lanes] - m).astype(BF16)
        acc = jnp.dot(vt_ref[:, pl.ds(start, span)], p, preferred_element_type=F32)
        l = acc[HEAD_DIM:HEAD_DIM + 1] + jnp.exp2(sinks[g] - m)
        return acc[:HEAD_DIM] / l

    bufs = ((qaug_a, s_a), (qaug_b, s_b))
    maxes = None
    for t in range(WINDOW_TILES + 1):
        new_maxes, outs = [], []
        if t < WINDOW_TILES:
            q0, start = window(t)
            in_band = jnp.abs(rel0 + (start - q0)) <= WINDOW
            bias_ref[...] = jnp.where(in_band, 0.0, -jnp.inf)
        for g in range(group):
            if t < WINDOW_TILES:
                new_maxes.append(tile_scores(t, g, *bufs[t % 2]))
            if t > 0:
                outs.append(tile_finish(t - 1, g, bufs[(t - 1) % 2][1], maxes[g]))
        if t > 0:
            o_ref[(t - 1) * tile:t * tile, :] = jnp.concatenate(outs, axis=0).T.astype(BF16)
        maxes = new_maxes


def _attn_params():
    return pltpu.CompilerParams(dimension_semantics=("parallel", "parallel", "arbitrary"),
                                vmem_limit_bytes=VMEM_LIMIT)


def _flash_scratch(key_chunk, n, dv):
    return [pltpu.VMEM((K_LANES, n), BF16), pltpu.VMEM((key_chunk, n), BF16),
            pltpu.VMEM((key_chunk, n), BF16), pltpu.VMEM((dv + SUM_ROWS, n), F32)]


def _attention_a(qt, k, vt, batch, seq):
    tq = 2 * Q_TILE
    nq = seq // tq
    group = A_HEADS // A_KV_HEADS
    width = group * HEAD_DIM
    key_chunk = min(KEY_CHUNK, seq // 2)
    return pl.pallas_call(
        functools.partial(_attn_a_kernel, key_chunk=key_chunk),
        grid=(batch, A_KV_HEADS, nq),
        scratch_shapes=_flash_scratch(key_chunk, group * tq, HEAD_DIM),
        in_specs=[
            pl.BlockSpec((width, tq), lambda b, j, i: (j, b * nq + i)),
            pl.BlockSpec((seq, K_LANES), lambda b, j, i: (b, j)),
            pl.BlockSpec((HEAD_DIM + SUM_ROWS, seq), lambda b, j, i: (j, b)),
        ],
        out_specs=pl.BlockSpec((tq, width), lambda b, j, i: (b * nq + i, j)),
        out_shape=jax.ShapeDtypeStruct((batch * seq, A_Q), BF16),
        compiler_params=_attn_params(), name="attn_axial")(qt, k, vt)


def _attention_b(qt, k, vt, lam_w, lam_init, batch, seq):
    tq = 4 * Q_TILE
    nq = seq // tq
    key_chunk = min(KEY_CHUNK, seq // 2)
    return pl.pallas_call(
        functools.partial(_attn_b_kernel, key_chunk=key_chunk, lam_init=lam_init),
        grid=(batch, B_HEADS, nq),
        scratch_shapes=_flash_scratch(key_chunk, 2 * tq, B_V_DIM),
        in_specs=[
            pl.BlockSpec((2 * HEAD_DIM, tq), lambda b, j, i: (j, b * nq + i)),
            pl.BlockSpec((seq, K_LANES), lambda b, j, i: (b, 2 * j)),
            pl.BlockSpec((seq, K_LANES), lambda b, j, i: (b, 2 * j + 1)),
            pl.BlockSpec((B_V_DIM + SUM_ROWS, seq), lambda b, j, i: (j, b)),
            pl.BlockSpec((4, HEAD_DIM), lambda b, j, i: (0, 0)),
        ],
        out_specs=pl.BlockSpec((tq, B_V_DIM), lambda b, j, i: (b * nq + i, j)),
        out_shape=jax.ShapeDtypeStruct((batch * seq, B_V), BF16),
        compiler_params=_attn_params(), name="attn_diff")(qt, k, k, vt, lam_w)


def _attention_c(qt, k, vt, sink, batch, seq):
    tq = WINDOW_TILES * Q_TILE
    nq = seq // tq
    group = C_HEADS // C_KV_HEADS
    width = group * HEAD_DIM
    span = Q_TILE + 2 * WINDOW
    return pl.pallas_call(
        _attn_c_kernel,
        grid=(batch, C_KV_HEADS, nq),
        scratch_shapes=[pltpu.VMEM((K_LANES, group * Q_TILE), BF16), pltpu.VMEM((K_LANES, group * Q_TILE), BF16),
                        pltpu.VMEM((span, group * Q_TILE), F32), pltpu.VMEM((span, group * Q_TILE), F32),
                        pltpu.VMEM((span, Q_TILE), F32)],
        in_specs=[
            pl.BlockSpec(memory_space=pltpu.SMEM),
            pl.BlockSpec((width, tq), lambda b, j, i: (j, b * nq + i)),
            pl.BlockSpec((seq, K_LANES), lambda b, j, i: (b, j)),
            pl.BlockSpec((HEAD_DIM + SUM_ROWS, seq), lambda b, j, i: (j, b)),
        ],
        out_specs=pl.BlockSpec((tq, width), lambda b, j, i: (b * nq + i, j)),
        out_shape=jax.ShapeDtypeStruct((batch * seq, C_Q), BF16),
        compiler_params=_attn_params(), name="attn_window")(sink, qt, k, vt)


def _post_kernel(lo_ref, hi_ref, x_ref, wo_ref, gmp_ref, gfp_ref, wgu_ref, wd_ref, gfo_ref, y_ref):
    half = lo_ref.shape[1]
    mix = (jnp.dot(lo_ref[...], wo_ref[0:half, :], preferred_element_type=F32)
           + jnp.dot(hi_ref[...], wo_ref[half:, :], preferred_element_type=F32))
    x = x_ref[...] + _rms(mix, gmp_ref[...])
    h = _rms(x, gfp_ref[...]).astype(BF16)
    acc = jnp.zeros(x.shape, F32)
    for c in range(FFN_HIDDEN // FFN_CHUNK):
        lo = c * FFN_CHUNK
        g = jnp.dot(h, wgu_ref[:, lo:lo + FFN_CHUNK], preferred_element_type=F32)
        u = jnp.dot(h, wgu_ref[:, FFN_HIDDEN + lo:FFN_HIDDEN + lo + FFN_CHUNK], preferred_element_type=F32)
        a = (g * (1.0 / (1.0 + jnp.exp(-g))) * u).astype(BF16)
        acc = acc + jnp.dot(a, wd_ref[lo:lo + FFN_CHUNK, :], preferred_element_type=F32)
    y_ref[...] = x + _rms(acc, gfo_ref[...])


def _post_attention(o_lo, o_hi, lo_blk, hi_blk, x, w_out, g_mix_post, g_ffn_pre, w_gu, w_down, g_ffn_post):
    tokens = x.shape[0]
    tm = TOKEN_TILE
    half = D_MODEL // 2
    row = lambda v: v.reshape(1, D_MODEL)
    return pl.pallas_call(
        _post_kernel, grid=(tokens // tm,),
        in_specs=[
            pl.BlockSpec((tm, half), lambda i: (i, lo_blk)),
            pl.BlockSpec((tm, half), lambda i: (i, hi_blk)),
            pl.BlockSpec((tm, D_MODEL), lambda i: (i, 0)),
            _resident(w_out.shape), _resident((1, D_MODEL)), _resident((1, D_MODEL)),
            _resident(w_gu.shape), _resident(w_down.shape), _resident((1, D_MODEL)),
        ],
        out_specs=pl.BlockSpec((tm, D_MODEL), lambda i: (i, 0)),
        out_shape=jax.ShapeDtypeStruct((tokens, D_MODEL), F32),
        compiler_params=_token_params(), name="out_proj_ffn")(
            o_lo, o_hi, x, w_out, row(g_mix_post), row(g_ffn_pre), w_gu, w_down, row(g_ffn_post))


def _trunk(x3, weights, tab):
    batch, seq, _ = x3.shape
    assert seq % TOKEN_TILE == 0 and seq % (4 * Q_TILE) == 0 and seq >= Q_TILE + 2 * WINDOW
    x = x3.reshape(batch * seq, D_MODEL)
    tiles_per_seq = seq // TOKEN_TILE
    pos_tile = lambda i: i % tiles_per_seq
    for l, w in enumerate(weights):
        if l % 2 == 0:
            qa, ka, va, qb, kb, vb = _input_projection(x, w["g_mix_pre"], w["wt_in"], tab, pos_tile, w["qk_gain"])
            o_lo = _attention_a(qa, ka, va, batch, seq)
            o_hi = _attention_b(qb, kb, vb, w["lam_w"], w["lam_init"], batch, seq)
            lo_blk, hi_blk = 0, 0
        else:
            q, k, v = _input_projection(x, w["g_mix_pre"], w["wt_in"], tab, pos_tile)
            o_lo = o_hi = _attention_c(q, k, v, w["sink"], batch, seq)
            lo_blk, hi_blk = 0, 1
        x = _post_attention(o_lo, o_hi, lo_blk, hi_blk, x, w["w_out"], w["g_mix_post"], w["g_ffn_pre"],
                            w["w_gu"], w["w_down"], w["g_ffn_post"])
    return x.reshape(batch, seq, D_MODEL)


def kernel(x_prompt, x_sample, w_in_even, w_out_even, qk_norm_a, diff_lambda, w_in_odd, w_out_odd,
           sink_c, w_gate_up, w_down, norm_mix_pre, norm_mix_post, norm_ffn_pre, norm_ffn_post):
    weights = []
    for l in range(w_gate_up.shape[0]):
        e = l // 2
        w = dict(g_mix_pre=norm_mix_pre[l], g_mix_post=norm_mix_post[l], g_ffn_pre=norm_ffn_pre[l],
                 g_ffn_post=norm_ffn_post[l], w_gu=w_gate_up[l].astype(BF16), w_down=w_down[l].astype(BF16))
        if l % 2 == 0:
            w.update(wt_in=w_in_even[e].T.astype(BF16), w_out=w_out_even[e].astype(BF16),
                     qk_gain=qk_norm_a[e], lam_w=diff_lambda[e], lam_init=0.8 - 0.6 * math.exp(-0.3 * l))
        else:
            w.update(wt_in=w_in_odd[e].T.astype(BF16), w_out=w_out_odd[e].astype(BF16), sink=sink_c[e])
        weights.append(w)
    tab = _position_table(max(x_prompt.shape[1], x_sample.shape[1]))
    return (_trunk(x_prompt, weights, tab), _trunk(x_sample, weights, tab))
```

```python
import functools
import math

import jax
import jax.numpy as jnp
from jax import lax
from jax.experimental import pallas as pl
from jax.experimental.pallas import tpu as pltpu

F32 = jnp.float32
BF16 = jnp.bfloat16

D_MODEL = 1024
HEAD_DIM = 64
GRID_W = 64
EPS = 1e-6
A_HEADS = 8
A_KV_HEADS = 2
AXIAL_THETA = 10000.0
B_HEADS = 4
B_V_DIM = 2 * HEAD_DIM
C_HEADS = 16
C_KV_HEADS = 4
WINDOW = 128
ROPE_THETA = 500000.0
ROPE_DIM = HEAD_DIM // 4
FFN_HIDDEN = 2816

A_Q = A_HEADS * HEAD_DIM
A_KV = A_KV_HEADS * HEAD_DIM
B_QK = B_HEADS * 2 * HEAD_DIM
B_V = B_HEADS * B_V_DIM
EVEN_IN = A_Q + 2 * A_KV + 2 * B_QK + B_V
C_Q = C_HEADS * HEAD_DIM
C_KV = C_KV_HEADS * HEAD_DIM
ODD_IN = C_Q + 2 * C_KV

LOG2E = 1.4426950408889634
Q_SCALE = HEAD_DIM ** -0.5 * LOG2E

TOKEN_TILE = 512
Q_TILE = 256
LANE_TILE = 256
WINDOW_TILES = 4
KEY_CHUNK = 256
STEPS_PER_ITER = 8
FFN_CHUNK = 256
VMEM_LIMIT = 56 * 1024 * 1024
K_LANES = 128
SUM_ROWS = 16
REF_ROW = HEAD_DIM
OVERFLOW_GUARD = 100.0

_TAB_ROWS = 80


def _rope_table(pos, dim, theta):
    inv = theta ** (-jnp.arange(0, dim, 2, dtype=F32) / dim)
    ang = pos.astype(F32)[:, None] * inv[None, :]
    return jnp.cos(ang), jnp.sin(ang)


def _position_table(seq):
    rows = seq // GRID_W
    t_row = jnp.broadcast_to(jnp.arange(rows)[:, None], (rows, GRID_W)).reshape(-1)
    t_col = jnp.broadcast_to(jnp.arange(GRID_W)[None, :], (rows, GRID_W)).reshape(-1)
    row_cos, row_sin = _rope_table(t_row, HEAD_DIM // 2, AXIAL_THETA)
    col_cos, col_sin = _rope_table(t_col, HEAD_DIM // 2, AXIAL_THETA)
    cos, sin = _rope_table(jnp.arange(seq), ROPE_DIM, ROPE_THETA)
    return jnp.concatenate([row_cos, row_sin, col_cos, col_sin, cos, sin], axis=1).T


def _rms(x, g):
    return x * lax.rsqrt(jnp.mean(x * x, axis=-1, keepdims=True) + EPS) * g


def _axial_rope_t(t, tab):
    rc, rs, cc, cs = tab[0:16][None], tab[16:32][None], tab[32:48][None], tab[48:64][None]
    x1, x2, x3, x4 = t[:, 0:16], t[:, 16:32], t[:, 32:48], t[:, 48:64]
    return jnp.concatenate(
        [x1 * rc - x2 * rs, x2 * rc + x1 * rs, x3 * cc - x4 * cs, x4 * cc + x3 * cs], axis=1)


def _partial_rope_t(t, tab):
    c, s = tab[64:72][None], tab[72:80][None]
    x1, x2 = t[:, 0:8], t[:, 8:16]
    return jnp.concatenate([x1 * c - x2 * s, x2 * c + x1 * s, t[:, 16:]], axis=1)


def _head_rms_t(t, g):
    r = lax.rsqrt(jnp.mean(t * t, axis=1, keepdims=True) + EPS)
    return t * r * g[None]


def _unit_row_block(heads, rows, n):
    return (lax.broadcasted_iota(jnp.int32, (heads, rows, n), 1) == 0).astype(F32)


def _with_ones_column(t):
    heads, d, n = t.shape
    aug = jnp.concatenate([t, _unit_row_block(heads, K_LANES - d, n)], axis=1)
    return aug.reshape(heads * K_LANES, n)


def _with_ones_rows(t):
    heads, dv, n = t.shape
    aug = jnp.concatenate([t, _unit_row_block(heads, SUM_ROWS, n)], axis=1)
    return aug.reshape(heads * (dv + SUM_ROWS), n)


def _proj_t(wt_ref, lo, hi, h):
    return lax.dot_general(wt_ref[lo:hi, :], h, (((1,), (1,)), ((), ())),
                           preferred_element_type=F32)


def _even_in_kernel(x_ref, g_ref, wt_ref, tab_ref, qkg_ref,
                    qa_ref, ka_ref, va_ref, qb_ref, kb_ref, vb_ref):
    n = x_ref.shape[0]
    h = _rms(x_ref[...], g_ref[...]).astype(BF16)
    tab = tab_ref[...]
    o = 0
    t = _proj_t(wt_ref, o, o + A_Q, h).reshape(A_HEADS, HEAD_DIM, n)
    t = _axial_rope_t(_head_rms_t(t, qkg_ref[0]), tab) * Q_SCALE
    qa_ref[...] = t.reshape(A_Q, n).astype(BF16)
    o += A_Q
    t = _proj_t(wt_ref, o, o + A_KV, h).reshape(A_KV_HEADS, HEAD_DIM, n)
    t = _axial_rope_t(_head_rms_t(t, qkg_ref[1]), tab)
    ka_ref[...] = _with_ones_column(t).T.astype(BF16)
    o += A_KV
    t = _proj_t(wt_ref, o, o + A_KV, h).reshape(A_KV_HEADS, HEAD_DIM, n)
    va_ref[...] = _with_ones_rows(t).astype(BF16)
    o += A_KV
    t = _proj_t(wt_ref, o, o + B_QK, h).reshape(2 * B_HEADS, HEAD_DIM, n)
    qb_ref[...] = (_partial_rope_t(t, tab) * Q_SCALE).reshape(B_QK, n).astype(BF16)
    o += B_QK
    t = _proj_t(wt_ref, o, o + B_QK, h).reshape(2 * B_HEADS, HEAD_DIM, n)
    kb_ref[...] = _with_ones_column(_partial_rope_t(t, tab)).T.astype(BF16)
    o += B_QK
    t = _proj_t(wt_ref, o, o + B_V, h).reshape(B_HEADS, B_V_DIM, n)
    vb_ref[...] = _with_ones_rows(t).astype(BF16)


def _odd_in_kernel(x_ref, g_ref, wt_ref, tab_ref, q_ref, k_ref, v_ref):
    n = x_ref.shape[0]
    h = _rms(x_ref[...], g_ref[...]).astype(BF16)
    tab = tab_ref[...]
    t = _proj_t(wt_ref, 0, C_Q, h).reshape(C_HEADS, HEAD_DIM, n)
    q_ref[...] = (_partial_rope_t(t, tab) * Q_SCALE).reshape(C_Q, n).astype(BF16)
    t = _proj_t(wt_ref, C_Q, C_Q + C_KV, h).reshape(C_KV_HEADS, HEAD_DIM, n)
    k_ref[...] = _with_ones_column(_partial_rope_t(t, tab)).T.astype(BF16)
    t = _proj_t(wt_ref, C_Q + C_KV, ODD_IN, h).reshape(C_KV_HEADS, HEAD_DIM, n)
    v_ref[...] = _with_ones_rows(t).astype(BF16)


def _resident(shape):
    nd = len(shape)
    return pl.BlockSpec(shape, lambda *_: (0,) * nd, pipeline_mode=pl.Buffered(1))


def _token_params():
    return pltpu.CompilerParams(dimension_semantics=("parallel",), vmem_limit_bytes=VMEM_LIMIT)


def _input_projection(x, gain, wt, tab, pos_tile, qk_gain=None):
    tokens = x.shape[0]
    tm = TOKEN_TILE
    grid = (tokens // tm,)
    in_specs = [
        pl.BlockSpec((tm, D_MODEL), lambda i: (i, 0)),
        _resident((1, D_MODEL)),
        _resident(wt.shape),
        pl.BlockSpec((_TAB_ROWS, tm), lambda i: (0, pos_tile(i))),
    ]
    args = [x, gain.reshape(1, D_MODEL), wt, tab]

    def feat(rows):
        return (jax.ShapeDtypeStruct((rows, tokens), BF16), pl.BlockSpec((rows, tm), lambda i: (0, i)))

    def tokm(cols):
        return (jax.ShapeDtypeStruct((tokens, cols), BF16), pl.BlockSpec((tm, cols), lambda i: (i, 0)))

    if qk_gain is not None:
        in_specs.append(_resident((2, HEAD_DIM, 1)))
        args.append(qk_gain.reshape(2, HEAD_DIM, 1))
        outs = [feat(A_Q), tokm(A_KV_HEADS * K_LANES), feat(A_KV_HEADS * (HEAD_DIM + SUM_ROWS)),
                feat(B_QK), tokm(2 * B_HEADS * K_LANES), feat(B_HEADS * (B_V_DIM + SUM_ROWS))]
        body, name = _even_in_kernel, "even_in_proj"
    else:
        outs = [feat(C_Q), tokm(C_KV_HEADS * K_LANES), feat(C_KV_HEADS * (HEAD_DIM + SUM_ROWS))]
        body, name = _odd_in_kernel, "odd_in_proj"
    return pl.pallas_call(
        body, grid=grid, in_specs=in_specs,
        out_specs=[o[1] for o in outs], out_shape=[o[0] for o in outs],
        compiler_params=_token_params(), name=name)(*args)


def _flash_t(k_refs, vt_ref, qaug_ref, s_a, s_b, acc_ref, key_chunk):
    n = qaug_ref.shape[1]
    width = n // len(k_refs)
    nc = k_refs[0].shape[0] // key_chunk
    assert nc >= 2 and STEPS_PER_ITER % 2 == 0
    ref_rows = lax.broadcasted_iota(jnp.int32, (SUM_ROWS, n), 0) == 0

    def bf16_round(x):
        return x.astype(BF16).astype(F32)

    def set_reference(ref):
        qaug_ref[REF_ROW:REF_ROW + SUM_ROWS, :] = jnp.where(ref_rows, -ref, 0.0).astype(BF16)

    def tile_scores(t, off, rows, s_ref):
        lanes = slice(t * LANE_TILE, (t + 1) * LANE_TILE)
        k_ref = k_refs[t * LANE_TILE // width]
        s = jnp.dot(k_ref[pl.ds(off, rows), :], qaug_ref[:, lanes], preferred_element_type=F32)
        if s_ref is not None:
            s_ref[:, lanes] = s.astype(BF16)
        return jnp.max(s, axis=0, keepdims=True)

    def tile_accumulate(t, off, s_ref, alpha):
        lanes = slice(t * LANE_TILE, (t + 1) * LANE_TILE)
        p = jnp.exp2(s_ref[:, lanes])
        pv = jnp.dot(vt_ref[:, pl.ds(off, key_chunk)], p, preferred_element_type=F32)
        acc_ref[:, lanes] = alpha[:, lanes] * acc_ref[:, lanes] + pv

    def step(c_next, s_next, c_cur, s_cur, alpha):
        maxes = []
        for t in range(n // LANE_TILE):
            if c_next is not None:
                maxes.append(tile_scores(t, pl.multiple_of(c_next * key_chunk, key_chunk), key_chunk, s_next))
            if c_cur is not None:
                tile_accumulate(t, pl.multiple_of(c_cur * key_chunk, key_chunk), s_cur, alpha)
        return jnp.concatenate(maxes, axis=1) if maxes else None

    def sweep(top0):
        bufs = (s_a, s_b)
        acc_ref[...] = jnp.zeros(acc_ref.shape, F32)
        ref0 = bf16_round(top0)
        set_reference(ref0)
        mc0 = step(0, s_a, None, None, None)

        def overlapped_step(k, parity, state):
            ref_pp, ref_p, mc_p, top, over = state
            ref_k = bf16_round(top)
            set_reference(ref_k)
            mc_k = step(k, bufs[parity], k - 1, bufs[1 - parity], jnp.exp2(ref_pp - ref_p))
            return ref_p, ref_k, mc_k, jnp.maximum(top, ref_p + mc_p), jnp.maximum(over, mc_p)

        def body(i, state):
            for u in range(STEPS_PER_ITER):
                state = overlapped_step(STEPS_PER_ITER * i + 1 + u, (1 + u) % 2, state)
            return state

        state = (ref0, ref0, mc0, top0, jnp.zeros((1, n), F32))
        iters = (nc - 1) // STEPS_PER_ITER
        state = lax.fori_loop(0, iters, body, state)
        for k in range(iters * STEPS_PER_ITER + 1, nc):
            state = overlapped_step(k, k % 2, state)
        ref_pp, ref_p, mc_p, top, over = state
        step(None, None, nc - 1, bufs[(nc - 1) % 2], jnp.exp2(ref_pp - ref_p))
        return jnp.maximum(top, ref_p + mc_p), jnp.maximum(over, mc_p)

    set_reference(jnp.zeros((1, n), F32))
    first = jnp.concatenate([tile_scores(t, 0, K_LANES, None) for t in range(n // LANE_TILE)], axis=1)
    top, over = sweep(first)

    @pl.when(jnp.max(over) > OVERFLOW_GUARD)
    def _():
        sweep(top)


def _finish(acc_ref, dv):
    acc = acc_ref[...]
    return acc[:dv] / acc[dv:dv + 1]


def _heads_to_token_major(o, heads, tq):
    stacked = jnp.concatenate([o[:, g * tq:(g + 1) * tq] for g in range(heads)], axis=0)
    return stacked.T


def _fill_queries(qaug_ref, q_ref, heads, tq):
    n = qaug_ref.shape[1]
    for g in range(heads):
        qaug_ref[0:HEAD_DIM, g * tq:(g + 1) * tq] = q_ref[g * HEAD_DIM:(g + 1) * HEAD_DIM, :]
    qaug_ref[REF_ROW + SUM_ROWS:, :] = jnp.zeros((K_LANES - REF_ROW - SUM_ROWS, n), BF16)


def _attn_a_kernel(q_ref, k_ref, vt_ref, o_ref, qaug_ref, s_a, s_b, acc_ref, *, key_chunk):
    tq = q_ref.shape[1]
    group = A_HEADS // A_KV_HEADS
    _fill_queries(qaug_ref, q_ref, group, tq)
    _flash_t([k_ref], vt_ref, qaug_ref, s_a, s_b, acc_ref, key_chunk)
    o_ref[...] = _heads_to_token_major(_finish(acc_ref, HEAD_DIM), group, tq).astype(BF16)


def _attn_b_kernel(q_ref, k0_ref, k1_ref, vt_ref, lam_ref, o_ref, qaug_ref, s_a, s_b, acc_ref, *,
                   key_chunk, lam_init):
    tq = q_ref.shape[1]
    _fill_queries(qaug_ref, q_ref, 2, tq)
    _flash_t([k0_ref, k1_ref], vt_ref, qaug_ref, s_a, s_b, acc_ref, key_chunk)
    lf = lam_ref[...]
    lam = (jnp.exp(jnp.sum(lf[0:1] * lf[1:2], axis=1, keepdims=True))
           - jnp.exp(jnp.sum(lf[2:3] * lf[3:4], axis=1, keepdims=True)) + lam_init)
    o = _finish(acc_ref, B_V_DIM)
    o = o[:, :tq] - lam * o[:, tq:]
    o = o * lax.rsqrt(jnp.mean(o * o, axis=0, keepdims=True) + EPS) * (1.0 - lam_init)
    o_ref[...] = o.T.astype(BF16)


def _attn_c_kernel(sink_ref, q_ref, k_ref, vt_ref, o_ref, qaug_a, qaug_b, s_a, s_b, bias_ref):
    group = C_HEADS // C_KV_HEADS
    seq = k_ref.shape[0]
    tile = q_ref.shape[1] // WINDOW_TILES
    span = tile + 2 * WINDOW
    j = pl.program_id(1)
    q_base = pl.program_id(2) * q_ref.shape[1]
    rel0 = (lax.broadcasted_iota(jnp.int32, (span, tile), 0)
            - lax.broadcasted_iota(jnp.int32, (span, tile), 1))
    sinks = [jnp.full((1, tile), sink_ref[j * group + g] * LOG2E, F32) for g in range(group)]
    for qaug in (qaug_a, qaug_b):
        qaug[HEAD_DIM:, :] = jnp.zeros((K_LANES - HEAD_DIM, group * tile), BF16)

    def window(t):
        q0 = q_base + t * tile
        return q0, pl.multiple_of(jnp.clip(q0 - WINDOW, 0, seq - span), WINDOW)

    def tile_scores(t, g, qaug, s_ref):
        _, start = window(t)
        lanes = slice(g * tile, (g + 1) * tile)
        qaug[0:HEAD_DIM, lanes] = q_ref[g * HEAD_DIM:(g + 1) * HEAD_DIM, t * tile:(t + 1) * tile]
        s = jnp.dot(k_ref[pl.ds(start, span), :], qaug[:, lanes], preferred_element_type=F32)
        s = s + bias_ref[...]
        s_ref[:, lanes] = s
        return jnp.maximum(jnp.max(s, axis=0, keepdims=True), sinks[g])

    def tile_finish(t, g, s_ref, m):
        _, start = window(t)
        lanes = slice(g * tile, (g + 1) * tile)
        p = jnp.exp2(s_ref[:, lanes] - m).astype(BF16)
        acc = jnp.dot(vt_ref[:, pl.ds(start, span)], p, preferred_element_type=F32)
        l = acc[HEAD_DIM:HEAD_DIM + 1] + jnp.exp2(sinks[g] - m)
        return acc[:HEAD_DIM] / l

    bufs = ((qaug_a, s_a), (qaug_b, s_b))
    maxes = None
    for t in range(WINDOW_TILES + 1):
        new_maxes, outs = [], []
        if t < WINDOW_TILES:
            q0, start = window(t)
            in_band = jnp.abs(rel0 + (start - q0)) <= WINDOW
            bias_ref[...] = jnp.where(in_band, 0.0, -jnp.inf)
        for g in range(group):
            if t < WINDOW_TILES:
                new_maxes.append(tile_scores(t, g, *bufs[t % 2]))
            if t > 0:
                outs.append(tile_finish(t - 1, g, bufs[(t - 1) % 2][1], maxes[g]))
        if t > 0:
            o_ref[(t - 1) * tile:t * tile, :] = jnp.concatenate(outs, axis=0).T.astype(BF16)
        maxes = new_maxes


def _attn_params():
    return pltpu.CompilerParams(dimension_semantics=("parallel", "parallel", "arbitrary"),
                                vmem_limit_bytes=VMEM_LIMIT)


def _flash_scratch(key_chunk, n, dv):
    return [pltpu.VMEM((K_LANES, n), BF16), pltpu.VMEM((key_chunk, n), BF16),
            pltpu.VMEM((key_chunk, n), BF16), pltpu.VMEM((dv + SUM_ROWS, n), F32)]


def _attention_a(qt, k, vt, batch, seq):
    tq = 2 * Q_TILE
    nq = seq // tq
    group = A_HEADS // A_KV_HEADS
    width = group * HEAD_DIM
    key_chunk = min(KEY_CHUNK, seq // 2)
    return pl.pallas_call(
        functools.partial(_attn_a_kernel, key_chunk=key_chunk),
        grid=(batch, A_KV_HEADS, nq),
        scratch_shapes=_flash_scratch(key_chunk, group * tq, HEAD_DIM),
        in_specs=[
            pl.BlockSpec((width, tq), lambda b, j, i: (j, b * nq + i)),
            pl.BlockSpec((seq, K_LANES), lambda b, j, i: (b, j)),
            pl.BlockSpec((HEAD_DIM + SUM_ROWS, seq), lambda b, j, i: (j, b)),
        ],
        out_specs=pl.BlockSpec((tq, width), lambda b, j, i: (b * nq + i, j)),
        out_shape=jax.ShapeDtypeStruct((batch * seq, A_Q), BF16),
        compiler_params=_attn_params(), name="attn_axial")(qt, k, vt)


def _attention_b(qt, k, vt, lam_w, lam_init, batch, seq):
    tq = 4 * Q_TILE
    nq = seq // tq
    key_chunk = min(KEY_CHUNK, seq // 2)
    return pl.pallas_call(
        functools.partial(_attn_b_kernel, key_chunk=key_chunk, lam_init=lam_init),
        grid=(batch, B_HEADS, nq),
        scratch_shapes=_flash_scratch(key_chunk, 2 * tq, B_V_DIM),
        in_specs=[
            pl.BlockSpec((2 * HEAD_DIM, tq), lambda b, j, i: (j, b * nq + i)),
            pl.BlockSpec((seq, K_LANES), lambda b, j, i: (b, 2 * j)),
            pl.BlockSpec((seq, K_LANES), lambda b, j, i: (b, 2 * j + 1)),
            pl.BlockSpec((B_V_DIM + SUM_ROWS, seq), lambda b, j, i: (j, b)),
            pl.BlockSpec((4, HEAD_DIM), lambda b, j, i: (0, 0)),
        ],
        out_specs=pl.BlockSpec((tq, B_V_DIM), lambda b, j, i: (b * nq + i, j)),
        out_shape=jax.ShapeDtypeStruct((batch * seq, B_V), BF16),
        compiler_params=_attn_params(), name="attn_diff")(qt, k, k, vt, lam_w)


def _attention_c(qt, k, vt, sink, batch, seq):
    tq = WINDOW_TILES * Q_TILE
    nq = seq // tq
    group = C_HEADS // C_KV_HEADS
    width = group * HEAD_DIM
    span = Q_TILE + 2 * WINDOW
    return pl.pallas_call(
        _attn_c_kernel,
        grid=(batch, C_KV_HEADS, nq),
        scratch_shapes=[pltpu.VMEM((K_LANES, group * Q_TILE), BF16), pltpu.VMEM((K_LANES, group * Q_TILE), BF16),
                        pltpu.VMEM((span, group * Q_TILE), F32), pltpu.VMEM((span, group * Q_TILE), F32),
                        pltpu.VMEM((span, Q_TILE), F32)],
        in_specs=[
            pl.BlockSpec(memory_space=pltpu.SMEM),
            pl.BlockSpec((width, tq), lambda b, j, i: (j, b * nq + i)),
            pl.BlockSpec((seq, K_LANES), lambda b, j, i: (b, j)),
            pl.BlockSpec((HEAD_DIM + SUM_ROWS, seq), lambda b, j, i: (j, b)),
        ],
        out_specs=pl.BlockSpec((tq, width), lambda b, j, i: (b * nq + i, j)),
        out_shape=jax.ShapeDtypeStruct((batch * seq, C_Q), BF16),
        compiler_params=_attn_params(), name="attn_window")(sink, qt, k, vt)


def _post_kernel(lo_ref, hi_ref, x_ref, wo_ref, gmp_ref, gfp_ref, wgu_ref, wd_ref, gfo_ref, y_ref):
    half = lo_ref.shape[1]
    mix = (jnp.dot(lo_ref[...], wo_ref[0:half, :], preferred_element_type=F32)
           + jnp.dot(hi_ref[...], wo_ref[half:, :], preferred_element_type=F32))
    x = x_ref[...] + _rms(mix, gmp_ref[...])
    h = _rms(x, gfp_ref[...]).astype(BF16)
    acc = jnp.zeros(x.shape, F32)
    for c in range(FFN_HIDDEN // FFN_CHUNK):
        lo = c * FFN_CHUNK
        g = jnp.dot(h, wgu_ref[:, lo:lo + FFN_CHUNK], preferred_element_type=F32)
        u = jnp.dot(h, wgu_ref[:, FFN_HIDDEN + lo:FFN_HIDDEN + lo + FFN_CHUNK], preferred_element_type=F32)
        a = (g * (1.0 / (1.0 + jnp.exp(-g))) * u).astype(BF16)
        acc = acc + jnp.dot(a, wd_ref[lo:lo + FFN_CHUNK, :], preferred_element_type=F32)
    y_ref[...] = x + _rms(acc, gfo_ref[...])


def _post_attention(o_lo, o_hi, lo_blk, hi_blk, x, w_out, g_mix_post, g_ffn_pre, w_gu, w_down, g_ffn_post):
    tokens = x.shape[0]
    tm = TOKEN_TILE
    half = D_MODEL // 2
    row = lambda v: v.reshape(1, D_MODEL)
    return pl.pallas_call(
        _post_kernel, grid=(tokens // tm,),
        in_specs=[
            pl.BlockSpec((tm, half), lambda i: (i, lo_blk)),
            pl.BlockSpec((tm, half), lambda i: (i, hi_blk)),
            pl.BlockSpec((tm, D_MODEL), lambda i: (i, 0)),
            _resident(w_out.shape), _resident((1, D_MODEL)), _resident((1, D_MODEL)),
            _resident(w_gu.shape), _resident(w_down.shape), _resident((1, D_MODEL)),
        ],
        out_specs=pl.BlockSpec((tm, D_MODEL), lambda i: (i, 0)),
        out_shape=jax.ShapeDtypeStruct((tokens, D_MODEL), F32),
        compiler_params=_token_params(), name="out_proj_ffn")(
            o_lo, o_hi, x, w_out, row(g_mix_post), row(g_ffn_pre), w_gu, w_down, row(g_ffn_post))


def _trunk(x3, weights, tab):
    batch, seq, _ = x3.shape
    assert seq % TOKEN_TILE == 0 and seq % (4 * Q_TILE) == 0 and seq >= Q_TILE + 2 * WINDOW
    x = x3.reshape(batch * seq, D_MODEL)
    tiles_per_seq = seq // TOKEN_TILE
    pos_tile = lambda i: i % tiles_per_seq
    for l, w in enumerate(weights):
        if l % 2 == 0:
            qa, ka, va, qb, kb, vb = _input_projection(x, w["g_mix_pre"], w["wt_in"], tab, pos_tile, w["qk_gain"])
            o_lo = _attention_a(qa, ka, va, batch, seq)
            o_hi = _attention_b(qb, kb, vb, w["lam_w"], w["lam_init"], batch, seq)
            lo_blk, hi_blk = 0, 0
        else:
            q, k, v = _input_projection(x, w["g_mix_pre"], w["wt_in"], tab, pos_tile)
            o_lo = o_hi = _attention_c(q, k, v, w["sink"], batch, seq)
            lo_blk, hi_blk = 0, 1
        x = _post_attention(o_lo, o_hi, lo_blk, hi_blk, x, w["w_out"], w["g_mix_post"], w["g_ffn_pre"],
                            w["w_gu"], w["w_down"], w["g_ffn_post"])
    return x.reshape(batch, seq, D_MODEL)


def kernel(x_prompt, x_sample, w_in_even, w_out_even, qk_norm_a, diff_lambda, w_in_odd, w_out_odd,
           sink_c, w_gate_up, w_down, norm_mix_pre, norm_mix_post, norm_ffn_pre, norm_ffn_post):
    weights = []
    for l in range(w_gate_up.shape[0]):
        e = l // 2
        w = dict(g_mix_pre=norm_mix_pre[l], g_mix_post=norm_mix_post[l], g_ffn_pre=norm_ffn_pre[l],
                 g_ffn_post=norm_ffn_post[l], w_gu=w_gate_up[l].astype(BF16), w_down=w_down[l].astype(BF16))
        if l % 2 == 0:
            w.update(wt_in=w_in_even[e].T.astype(BF16), w_out=w_out_even[e].astype(BF16),
                     qk_gain=qk_norm_a[e], lam_w=diff_lambda[e], lam_init=0.8 - 0.6 * math.exp(-0.3 * l))
        else:
            w.update(wt_in=w_in_odd[e].T.astype(BF16), w_out=w_out_odd[e].astype(BF16), sink=sink_c[e])
        weights.append(w)
    tab = _position_table(max(x_prompt.shape[1], x_sample.shape[1]))
    return (_trunk(x_prompt, weights, tab), _trunk(x_sample, weights, tab))
```

```python
import functools
import math

import jax
import jax.numpy as jnp
from jax import lax
from jax.experimental import pallas as pl
from jax.experimental.pallas import tpu as pltpu

F32 = jnp.float32
BF16 = jnp.bfloat16

D_MODEL = 1024
HEAD_DIM = 64
GRID_W = 64
EPS = 1e-6
A_HEADS = 8
A_KV_HEADS = 2
AXIAL_THETA = 10000.0
B_HEADS = 4
B_V_DIM = 2 * HEAD_DIM
C_HEADS = 16
C_KV_HEADS = 4
WINDOW = 128
ROPE_THETA = 500000.0
ROPE_DIM = HEAD_DIM // 4
FFN_HIDDEN = 2816

A_Q = A_HEADS * HEAD_DIM
A_KV = A_KV_HEADS * HEAD_DIM
B_QK = B_HEADS * 2 * HEAD_DIM
B_V = B_HEADS * B_V_DIM
EVEN_IN = A_Q + 2 * A_KV + 2 * B_QK + B_V
C_Q = C_HEADS * HEAD_DIM
C_KV = C_KV_HEADS * HEAD_DIM
ODD_IN = C_Q + 2 * C_KV

LOG2E = 1.4426950408889634
Q_SCALE = HEAD_DIM ** -0.5 * LOG2E

TOKEN_TILE = 512
Q_TILE = 256
LANE_TILE = 256
WINDOW_TILES = 4
KEY_CHUNK = 256
STEPS_PER_ITER = 8
FFN_CHUNK = 256
VMEM_LIMIT = 56 * 1024 * 1024
K_LANES = 128
SUM_ROWS = 16
REF_ROW = HEAD_DIM
OVERFLOW_GUARD = 100.0

_TAB_ROWS = 80


def _rope_table(pos, dim, theta):
    inv = theta ** (-jnp.arange(0, dim, 2, dtype=F32) / dim)
    ang = pos.astype(F32)[:, None] * inv[None, :]
    return jnp.cos(ang), jnp.sin(ang)


def _position_table(seq):
    rows = seq // GRID_W
    t_row = jnp.broadcast_to(jnp.arange(rows)[:, None], (rows, GRID_W)).reshape(-1)
    t_col = jnp.broadcast_to(jnp.arange(GRID_W)[None, :], (rows, GRID_W)).reshape(-1)
    row_cos, row_sin = _rope_table(t_row, HEAD_DIM // 2, AXIAL_THETA)
    col_cos, col_sin = _rope_table(t_col, HEAD_DIM // 2, AXIAL_THETA)
    cos, sin = _rope_table(jnp.arange(seq), ROPE_DIM, ROPE_THETA)
    return jnp.concatenate([row_cos, row_sin, col_cos, col_sin, cos, sin], axis=1).T


def _rms(x, g):
    return x * lax.rsqrt(jnp.mean(x * x, axis=-1, keepdims=True) + EPS) * g


def _axial_rope_t(t, tab):
    rc, rs, cc, cs = tab[0:16][None], tab[16:32][None], tab[32:48][None], tab[48:64][None]
    x1, x2, x3, x4 = t[:, 0:16], t[:, 16:32], t[:, 32:48], t[:, 48:64]
    return jnp.concatenate(
        [x1 * rc - x2 * rs, x2 * rc + x1 * rs, x3 * cc - x4 * cs, x4 * cc + x3 * cs], axis=1)


def _partial_rope_t(t, tab):
    c, s = tab[64:72][None], tab[72:80][None]
    x1, x2 = t[:, 0:8], t[:, 8:16]
    return jnp.concatenate([x1 * c - x2 * s, x2 * c + x1 * s, t[:, 16:]], axis=1)


def _head_rms_t(t, g):
    r = lax.rsqrt(jnp.mean(t * t, axis=1, keepdims=True) + EPS)
    return t * r * g[None]


def _unit_row_block(heads, rows, n):
    return (lax.broadcasted_iota(jnp.int32, (heads, rows, n), 1) == 0).astype(F32)


def _store_keys(k_ref, t):
    heads, d, n = t.shape
    aug = jnp.concatenate([t, _unit_row_block(heads, K_LANES - d, n)], axis=1)
    for hd in range(heads):
        k_ref[hd] = aug[hd].T.astype(BF16)


def _with_ones_rows(t):
    heads, dv, n = t.shape
    aug = jnp.concatenate([t, _unit_row_block(heads, SUM_ROWS, n)], axis=1)
    return aug.reshape(heads * (dv + SUM_ROWS), n)


def _proj_t(wt_ref, lo, hi, h):
    return lax.dot_general(wt_ref[lo:hi, :], h, (((1,), (1,)), ((), ())),
                           preferred_element_type=F32)


def _even_in_kernel(x_ref, g_ref, wt_ref, tab_ref, qkg_ref,
                    qa_ref, ka_ref, va_ref, qb_ref, kb_ref, vb_ref):
    n = x_ref.shape[0]
    h = _rms(x_ref[...], g_ref[...]).astype(BF16)
    tab = tab_ref[...]
    o = 0
    t = _proj_t(wt_ref, o, o + A_Q, h).reshape(A_HEADS, HEAD_DIM, n)
    t = _axial_rope_t(_head_rms_t(t, qkg_ref[0]), tab) * Q_SCALE
    qa_ref[...] = t.reshape(A_Q, n).astype(BF16)
    o += A_Q
    t = _proj_t(wt_ref, o, o + A_KV, h).reshape(A_KV_HEADS, HEAD_DIM, n)
    t = _axial_rope_t(_head_rms_t(t, qkg_ref[1]), tab)
    _store_keys(ka_ref, t)
    o += A_KV
    t = _proj_t(wt_ref, o, o + A_KV, h).reshape(A_KV_HEADS, HEAD_DIM, n)
    va_ref[...] = _with_ones_rows(t).astype(BF16)
    o += A_KV
    t = _proj_t(wt_ref, o, o + B_QK, h).reshape(2 * B_HEADS, HEAD_DIM, n)
    qb_ref[...] = (_partial_rope_t(t, tab) * Q_SCALE).reshape(B_QK, n).astype(BF16)
    o += B_QK
    t = _proj_t(wt_ref, o, o + B_QK, h).reshape(2 * B_HEADS, HEAD_DIM, n)
    _store_keys(kb_ref, _partial_rope_t(t, tab))
    o += B_QK
    t = _proj_t(wt_ref, o, o + B_V, h).reshape(B_HEADS, B_V_DIM, n)
    vb_ref[...] = _with_ones_rows(t).astype(BF16)


def _odd_in_kernel(x_ref, g_ref, wt_ref, tab_ref, q_ref, k_ref, v_ref):
    n = x_ref.shape[0]
    h = _rms(x_ref[...], g_ref[...]).astype(BF16)
    tab = tab_ref[...]
    t = _proj_t(wt_ref, 0, C_Q, h).reshape(C_HEADS, HEAD_DIM, n)
    q_ref[...] = (_partial_rope_t(t, tab) * Q_SCALE).reshape(C_Q, n).astype(BF16)
    t = _proj_t(wt_ref, C_Q, C_Q + C_KV, h).reshape(C_KV_HEADS, HEAD_DIM, n)
    _store_keys(k_ref, _partial_rope_t(t, tab))
    t = _proj_t(wt_ref, C_Q + C_KV, ODD_IN, h).reshape(C_KV_HEADS, HEAD_DIM, n)
    v_ref[...] = _with_ones_rows(t).astype(BF16)


def _resident(shape):
    nd = len(shape)
    return pl.BlockSpec(shape, lambda *_: (0,) * nd, pipeline_mode=pl.Buffered(1))


def _token_params():
    return pltpu.CompilerParams(dimension_semantics=("parallel",), vmem_limit_bytes=VMEM_LIMIT)


def _input_projection(x, gain, wt, tab, pos_tile, qk_gain=None):
    tokens = x.shape[0]
    tm = TOKEN_TILE
    grid = (tokens // tm,)
    in_specs = [
        pl.BlockSpec((tm, D_MODEL), lambda i: (i, 0)),
        _resident((1, D_MODEL)),
        _resident(wt.shape),
        pl.BlockSpec((_TAB_ROWS, tm), lambda i: (0, pos_tile(i))),
    ]
    args = [x, gain.reshape(1, D_MODEL), wt, tab]

    def feat(rows):
        return (jax.ShapeDtypeStruct((rows, tokens), BF16), pl.BlockSpec((rows, tm), lambda i: (0, i)))

    def keys(heads):
        return (jax.ShapeDtypeStruct((heads, tokens, K_LANES), BF16),
                pl.BlockSpec((heads, tm, K_LANES), lambda i: (0, i, 0)))

    if qk_gain is not None:
        in_specs.append(_resident((2, HEAD_DIM, 1)))
        args.append(qk_gain.reshape(2, HEAD_DIM, 1))
        outs = [feat(A_Q), keys(A_KV_HEADS), feat(A_KV_HEADS * (HEAD_DIM + SUM_ROWS)),
                feat(B_QK), keys(2 * B_HEADS), feat(B_HEADS * (B_V_DIM + SUM_ROWS))]
        body, name = _even_in_kernel, "even_in_proj"
    else:
        outs = [feat(C_Q), keys(C_KV_HEADS), feat(C_KV_HEADS * (HEAD_DIM + SUM_ROWS))]
        body, name = _odd_in_kernel, "odd_in_proj"
    return pl.pallas_call(
        body, grid=grid, in_specs=in_specs,
        out_specs=[o[1] for o in outs], out_shape=[o[0] for o in outs],
        compiler_params=_token_params(), name=name)(*args)


def _flash_t(k_refs, vt_ref, qaug_ref, s_a, s_b, acc_ref, key_chunk):
    n = qaug_ref.shape[1]
    width = n // len(k_refs)
    nc = k_refs[0].shape[0] // key_chunk
    assert nc >= 2 and STEPS_PER_ITER % 2 == 0
    ref_rows = lax.broadcasted_iota(jnp.int32, (SUM_ROWS, n), 0) == 0

    def bf16_round(x):
        return x.astype(BF16).astype(F32)

    def set_reference(ref):
        qaug_ref[REF_ROW:REF_ROW + SUM_ROWS, :] = jnp.where(ref_rows, -ref, 0.0).astype(BF16)

    def tile_scores(t, off, rows, s_ref):
        lanes = slice(t * LANE_TILE, (t + 1) * LANE_TILE)
        k_ref = k_refs[t * LANE_TILE // width]
        s = jnp.dot(k_ref[pl.ds(off, rows), :], qaug_ref[:, lanes], preferred_element_type=F32)
        if s_ref is not None:
            s_ref[:, lanes] = s.astype(BF16)
        return jnp.max(s, axis=0, keepdims=True)

    def tile_accumulate(t, off, s_ref, alpha):
        lanes = slice(t * LANE_TILE, (t + 1) * LANE_TILE)
        p = jnp.exp2(s_ref[:, lanes])
        pv = jnp.dot(vt_ref[:, pl.ds(off, key_chunk)], p, preferred_element_type=F32)
        acc_ref[:, lanes] = alpha[:, lanes] * acc_ref[:, lanes] + pv

    def step(c_next, s_next, c_cur, s_cur, alpha):
        maxes = []
        for t in range(n // LANE_TILE):
            if c_next is not None:
                maxes.append(tile_scores(t, pl.multiple_of(c_next * key_chunk, key_chunk), key_chunk, s_next))
            if c_cur is not None:
                tile_accumulate(t, pl.multiple_of(c_cur * key_chunk, key_chunk), s_cur, alpha)
        return jnp.concatenate(maxes, axis=1) if maxes else None

    def sweep(top0):
        bufs = (s_a, s_b)
        acc_ref[...] = jnp.zeros(acc_ref.shape, F32)
        ref0 = bf16_round(top0)
        set_reference(ref0)
        mc0 = step(0, s_a, None, None, None)

        def overlapped_step(k, parity, state):
            ref_pp, ref_p, mc_p, top, over = state
            ref_k = bf16_round(top)
            set_reference(ref_k)
            mc_k = step(k, bufs[parity], k - 1, bufs[1 - parity], jnp.exp2(ref_pp - ref_p))
            return ref_p, ref_k, mc_k, jnp.maximum(top, ref_p + mc_p), jnp.maximum(over, mc_p)

        def body(i, state):
            for u in range(STEPS_PER_ITER):
                state = overlapped_step(STEPS_PER_ITER * i + 1 + u, (1 + u) % 2, state)
            return state

        state = (ref0, ref0, mc0, top0, jnp.zeros((1, n), F32))
        iters = (nc - 1) // STEPS_PER_ITER
        state = lax.fori_loop(0, iters, body, state)
        for k in range(iters * STEPS_PER_ITER + 1, nc):
            state = overlapped_step(k, k % 2, state)
        ref_pp, ref_p, mc_p, top, over = state
        step(None, None, nc - 1, bufs[(nc - 1) % 2], jnp.exp2(ref_pp - ref_p))
        return jnp.maximum(top, ref_p + mc_p), jnp.maximum(over, mc_p)

    set_reference(jnp.zeros((1, n), F32))
    first = jnp.concatenate([tile_scores(t, 0, K_LANES, None) for t in range(n // LANE_TILE)], axis=1)
    top, over = sweep(first)

    @pl.when(jnp.max(over) > OVERFLOW_GUARD)
    def _():
        sweep(top)


def _finish(acc_ref, dv):
    acc = acc_ref[...]
    return acc[:dv] / acc[dv:dv + 1]


def _heads_to_token_major(o, heads, tq):
    stacked = jnp.concatenate([o[:, g * tq:(g + 1) * tq] for g in range(heads)], axis=0)
    return stacked.T


def _fill_queries(qaug_ref, q_ref, heads, tq):
    n = qaug_ref.shape[1]
    for g in range(heads):
        qaug_ref[0:HEAD_DIM, g * tq:(g + 1) * tq] = q_ref[g * HEAD_DIM:(g + 1) * HEAD_DIM, :]
    qaug_ref[REF_ROW + SUM_ROWS:, :] = jnp.zeros((K_LANES - REF_ROW - SUM_ROWS, n), BF16)


def _attn_a_kernel(q_ref, k_ref, vt_ref, o_ref, qaug_ref, s_a, s_b, acc_ref, *, key_chunk):
    tq = q_ref.shape[1]
    group = A_HEADS // A_KV_HEADS
    _fill_queries(qaug_ref, q_ref, group, tq)
    _flash_t([k_ref], vt_ref, qaug_ref, s_a, s_b, acc_ref, key_chunk)
    o_ref[...] = _heads_to_token_major(_finish(acc_ref, HEAD_DIM), group, tq).astype(BF16)


def _attn_b_kernel(q_ref, k0_ref, k1_ref, vt_ref, lam_ref, o_ref, qaug_ref, s_a, s_b, acc_ref, *,
                   key_chunk, lam_init):
    tq = q_ref.shape[1]
    _fill_queries(qaug_ref, q_ref, 2, tq)
    _flash_t([k0_ref, k1_ref], vt_ref, qaug_ref, s_a, s_b, acc_ref, key_chunk)
    lf = lam_ref[...]
    lam = (jnp.exp(jnp.sum(lf[0:1] * lf[1:2], axis=1, keepdims=True))
           - jnp.exp(jnp.sum(lf[2:3] * lf[3:4], axis=1, keepdims=True)) + lam_init)
    o = _finish(acc_ref, B_V_DIM)
    o = o[:, :tq] - lam * o[:, tq:]
    o = o * lax.rsqrt(jnp.mean(o * o, axis=0, keepdims=True) + EPS) * (1.0 - lam_init)
    o_ref[...] = o.T.astype(BF16)


def _attn_c_kernel(sink_ref, q_ref, k_ref, vt_ref, o_ref, qaug_a, qaug_b, s_a, s_b, bias_ref):
    group = C_HEADS // C_KV_HEADS
    seq = k_ref.shape[0]
    tile = q_ref.shape[1] // WINDOW_TILES
    span = tile + 2 * WINDOW
    j = pl.program_id(1)
    q_base = pl.program_id(2) * q_ref.shape[1]
    rel0 = (lax.broadcasted_iota(jnp.int32, (span, tile), 0)
            - lax.broadcasted_iota(jnp.int32, (span, tile), 1))
    sinks = [jnp.full((1, tile), sink_ref[j * group + g] * LOG2E, F32) for g in range(group)]
    for qaug in (qaug_a, qaug_b):
        qaug[HEAD_DIM:, :] = jnp.zeros((K_LANES - HEAD_DIM, group * tile), BF16)

    def window(t):
        q0 = q_base + t * tile
        return q0, pl.multiple_of(jnp.clip(q0 - WINDOW, 0, seq - span), WINDOW)

    def tile_scores(t, g, qaug, s_ref):
        _, start = window(t)
        lanes = slice(g * tile, (g + 1) * tile)
        qaug[0:HEAD_DIM, lanes] = q_ref[g * HEAD_DIM:(g + 1) * HEAD_DIM, t * tile:(t + 1) * tile]
        s = jnp.dot(k_ref[pl.ds(start, span), :], qaug[:, lanes], preferred_element_type=F32)
        s = s + bias_ref[...]
        s_ref[:, lanes] = s
        return jnp.maximum(jnp.max(s, axis=0, keepdims=True), sinks[g])

    def tile_finish(t, g, s_ref, m):
        _, start = window(t)
        lanes = slice(g * tile, (g + 1) * tile)
        p = jnp.exp2(s_ref[:, lanes] - m).astype(BF16)
        acc = jnp.dot(vt_ref[:, pl.ds(start, span)], p, preferred_element_type=F32)
        l = acc[HEAD_DIM:HEAD_DIM + 1] + jnp.exp2(sinks[g] - m)
        return acc[:HEAD_DIM] / l

    bufs = ((qaug_a, s_a), (qaug_b, s_b))
    maxes = None
    for t in range(WINDOW_TILES + 1):
        new_maxes, outs = [], []
        if t < WINDOW_TILES:
            q0, start = window(t)
            in_band = jnp.abs(rel0 + (start - q0)) <= WINDOW
            bias_ref[...] = jnp.where(in_band, 0.0, -jnp.inf)
        for g in range(group):
            if t < WINDOW_TILES:
                new_maxes.append(tile_scores(t, g, *bufs[t % 2]))
            if t > 0:
                outs.append(tile_finish(t - 1, g, bufs[(t - 1) % 2][1], maxes[g]))
        if t > 0:
            o_ref[(t - 1) * tile:t * tile, :] = jnp.concatenate(outs, axis=0).T.astype(BF16)
        maxes = new_maxes


def _attn_params():
    return pltpu.CompilerParams(dimension_semantics=("parallel", "parallel", "arbitrary"),
                                vmem_limit_bytes=VMEM_LIMIT)


def _flash_scratch(key_chunk, n, dv):
    return [pltpu.VMEM((K_LANES, n), BF16), pltpu.VMEM((key_chunk, n), BF16),
            pltpu.VMEM((key_chunk, n), BF16), pltpu.VMEM((dv + SUM_ROWS, n), F32)]


def _attention_a(qt, k, vt, batch, seq):
    tq = 2 * Q_TILE
    nq = seq // tq
    group = A_HEADS // A_KV_HEADS
    width = group * HEAD_DIM
    key_chunk = min(KEY_CHUNK, seq // 2)
    return pl.pallas_call(
        functools.partial(_attn_a_kernel, key_chunk=key_chunk),
        grid=(batch, A_KV_HEADS, nq),
        scratch_shapes=_flash_scratch(key_chunk, group * tq, HEAD_DIM),
        in_specs=[
            pl.BlockSpec((width, tq), lambda b, j, i: (j, b * nq + i)),
            pl.BlockSpec((None, seq, K_LANES), lambda b, j, i: (j, b, 0)),
            pl.BlockSpec((HEAD_DIM + SUM_ROWS, seq), lambda b, j, i: (j, b)),
        ],
        out_specs=pl.BlockSpec((tq, width), lambda b, j, i: (b * nq + i, j)),
        out_shape=jax.ShapeDtypeStruct((batch * seq, A_Q), BF16),
        compiler_params=_attn_params(), name="attn_axial")(qt, k, vt)


def _attention_b(qt, k, vt, lam_w, lam_init, batch, seq):
    tq = 4 * Q_TILE
    nq = seq // tq
    key_chunk = min(KEY_CHUNK, seq // 2)
    return pl.pallas_call(
        functools.partial(_attn_b_kernel, key_chunk=key_chunk, lam_init=lam_init),
        grid=(batch, B_HEADS, nq),
        scratch_shapes=_flash_scratch(key_chunk, 2 * tq, B_V_DIM),
        in_specs=[
            pl.BlockSpec((2 * HEAD_DIM, tq), lambda b, j, i: (j, b * nq + i)),
            pl.BlockSpec((None, seq, K_LANES), lambda b, j, i: (2 * j, b, 0)),
            pl.BlockSpec((None, seq, K_LANES), lambda b, j, i: (2 * j + 1, b, 0)),
            pl.BlockSpec((B_V_DIM + SUM_ROWS, seq), lambda b, j, i: (j, b)),
            pl.BlockSpec((4, HEAD_DIM), lambda b, j, i: (0, 0)),
        ],
        out_specs=pl.BlockSpec((tq, B_V_DIM), lambda b, j, i: (b * nq + i, j)),
        out_shape=jax.ShapeDtypeStruct((batch * seq, B_V), BF16),
        compiler_params=_attn_params(), name="attn_diff")(qt, k, k, vt, lam_w)


def _attention_c(qt, k, vt, sink, batch, seq):
    tq = WINDOW_TILES * Q_TILE
    nq = seq // tq
    group = C_HEADS // C_KV_HEADS
    width = group * HEAD_DIM
    span = Q_TILE + 2 * WINDOW
    return pl.pallas_call(
        _attn_c_kernel,
        grid=(batch, C_KV_HEADS, nq),
        scratch_shapes=[pltpu.VMEM((K_LANES, group * Q_TILE), BF16), pltpu.VMEM((K_LANES, group * Q_TILE), BF16),
                        pltpu.VMEM((span, group * Q_TILE), F32), pltpu.VMEM((span, group * Q_TILE), F32),
                        pltpu.VMEM((span, Q_TILE), F32)],
        in_specs=[
            pl.BlockSpec(memory_space=pltpu.SMEM),
            pl.BlockSpec((width, tq), lambda b, j, i: (j, b * nq + i)),
            pl.BlockSpec((None, seq, K_LANES), lambda b, j, i: (j, b, 0)),
            pl.BlockSpec((HEAD_DIM + SUM_ROWS, seq), lambda b, j, i: (j, b)),
        ],
        out_specs=pl.BlockSpec((tq, width), lambda b, j, i: (b * nq + i, j)),
        out_shape=jax.ShapeDtypeStruct((batch * seq, C_Q), BF16),
        compiler_params=_attn_params(), name="attn_window")(sink, qt, k, vt)


def _post_kernel(lo_ref, hi_ref, x_ref, wo_ref, gmp_ref, gfp_ref, wgu_ref, wd_ref, gfo_ref, y_ref):
    half = lo_ref.shape[1]
    mix = (jnp.dot(lo_ref[...], wo_ref[0:half, :], preferred_element_type=F32)
           + jnp.dot(hi_ref[...], wo_ref[half:, :], preferred_element_type=F32))
    x = x_ref[...] + _rms(mix, gmp_ref[...])
    h = _rms(x, gfp_ref[...]).astype(BF16)
    acc = jnp.zeros(x.shape, F32)
    for c in range(FFN_HIDDEN // FFN_CHUNK):
        lo = c * FFN_CHUNK
        g = jnp.dot(h, wgu_ref[:, lo:lo + FFN_CHUNK], preferred_element_type=F32)
        u = jnp.dot(h, wgu_ref[:, FFN_HIDDEN + lo:FFN_HIDDEN + lo + FFN_CHUNK], preferred_element_type=F32)
        a = (g * (1.0 / (1.0 + jnp.exp(-g))) * u).astype(BF16)
        acc = acc + jnp.dot(a, wd_ref[lo:lo + FFN_CHUNK, :], preferred_element_type=F32)
    y_ref[...] = x + _rms(acc, gfo_ref[...])


def _post_attention(o_lo, o_hi, lo_blk, hi_blk, x, w_out, g_mix_post, g_ffn_pre, w_gu, w_down, g_ffn_post):
    tokens = x.shape[0]
    tm = TOKEN_TILE
    half = D_MODEL // 2
    row = lambda v: v.reshape(1, D_MODEL)
    return pl.pallas_call(
        _post_kernel, grid=(tokens // tm,),
        in_specs=[
            pl.BlockSpec((tm, half), lambda i: (i, lo_blk)),
            pl.BlockSpec((tm, half), lambda i: (i, hi_blk)),
            pl.BlockSpec((tm, D_MODEL), lambda i: (i, 0)),
            _resident(w_out.shape), _resident((1, D_MODEL)), _resident((1, D_MODEL)),
            _resident(w_gu.shape), _resident(w_down.shape), _resident((1, D_MODEL)),
        ],
        out_specs=pl.BlockSpec((tm, D_MODEL), lambda i: (i, 0)),
        out_shape=jax.ShapeDtypeStruct((tokens, D_MODEL), F32),
        compiler_params=_token_params(), name="out_proj_ffn")(
            o_lo, o_hi, x, w_out, row(g_mix_post), row(g_ffn_pre), w_gu, w_down, row(g_ffn_post))


def _trunk(x3, weights, tab):
    batch, seq, _ = x3.shape
    assert seq % TOKEN_TILE == 0 and seq % (4 * Q_TILE) == 0 and seq >= Q_TILE + 2 * WINDOW
    x = x3.reshape(batch * seq, D_MODEL)
    tiles_per_seq = seq // TOKEN_TILE
    pos_tile = lambda i: i % tiles_per_seq
    for l, w in enumerate(weights):
        if l % 2 == 0:
            qa, ka, va, qb, kb, vb = _input_projection(x, w["g_mix_pre"], w["wt_in"], tab, pos_tile, w["qk_gain"])
            o_lo = _attention_a(qa, ka, va, batch, seq)
            o_hi = _attention_b(qb, kb, vb, w["lam_w"], w["lam_init"], batch, seq)
            lo_blk, hi_blk = 0, 0
        else:
            q, k, v = _input_projection(x, w["g_mix_pre"], w["wt_in"], tab, pos_tile)
            o_lo = o_hi = _attention_c(q, k, v, w["sink"], batch, seq)
            lo_blk, hi_blk = 0, 1
        x = _post_attention(o_lo, o_hi, lo_blk, hi_blk, x, w["w_out"], w["g_mix_post"], w["g_ffn_pre"],
                            w["w_gu"], w["w_down"], w["g_ffn_post"])
    return x.reshape(batch, seq, D_MODEL)


def kernel(x_prompt, x_sample, w_in_even, w_out_even, qk_norm_a, diff_lambda, w_in_odd, w_out_odd,
           sink_c, w_gate_up, w_down, norm_mix_pre, norm_mix_post, norm_ffn_pre, norm_ffn_post):
    weights = []
    for l in range(w_gate_up.shape[0]):
        e = l // 2
        w = dict(g_mix_pre=norm_mix_pre[l], g_mix_post=norm_mix_post[l], g_ffn_pre=norm_ffn_pre[l],
                 g_ffn_post=norm_ffn_post[l], w_gu=w_gate_up[l].astype(BF16), w_down=w_down[l].astype(BF16))
        if l % 2 == 0:
            w.update(wt_in=w_in_even[e].T.astype(BF16), w_out=w_out_even[e].astype(BF16),
                     qk_gain=qk_norm_a[e], lam_w=diff_lambda[e], lam_init=0.8 - 0.6 * math.exp(-0.3 * l))
        else:
            w.update(wt_in=w_in_odd[e].T.astype(BF16), w_out=w_out_odd[e].astype(BF16), sink=sink_c[e])
        weights.append(w)
    tab = _position_table(max(x_prompt.shape[1], x_sample.shape[1]))
    return (_trunk(x_prompt, weights, tab), _trunk(x_sample, weights, tab))
```

```python
import functools
import math

import jax
import jax.numpy as jnp
from jax import lax
from jax.experimental import pallas as pl
from jax.experimental.pallas import tpu as pltpu

F32 = jnp.float32
BF16 = jnp.bfloat16

D_MODEL = 1024
HEAD_DIM = 64
GRID_W = 64
EPS = 1e-6
A_HEADS = 8
A_KV_HEADS = 2
AXIAL_THETA = 10000.0
B_HEADS = 4
B_V_DIM = 2 * HEAD_DIM
C_HEADS = 16
C_KV_HEADS = 4
WINDOW = 128
ROPE_THETA = 500000.0
ROPE_DIM = HEAD_DIM // 4
FFN_HIDDEN = 2816

A_Q = A_HEADS * HEAD_DIM
A_KV = A_KV_HEADS * HEAD_DIM
B_QK = B_HEADS * 2 * HEAD_DIM
B_V = B_HEADS * B_V_DIM
EVEN_IN = A_Q + 2 * A_KV + 2 * B_QK + B_V
C_Q = C_HEADS * HEAD_DIM
C_KV = C_KV_HEADS * HEAD_DIM
ODD_IN = C_Q + 2 * C_KV

LOG2E = 1.4426950408889634
Q_SCALE = HEAD_DIM ** -0.5 * LOG2E

TOKEN_TILE = 512
Q_TILE = 256
LANE_TILE = 256
WINDOW_TILES = 4
KEY_CHUNK = 256
STEPS_PER_ITER = 8
FFN_CHUNK = 256
VMEM_LIMIT = 56 * 1024 * 1024
K_LANES = 128
SUM_ROWS = 16
REF_ROW = HEAD_DIM
OVERFLOW_GUARD = 100.0

_TAB_ROWS = 80


def _rope_table(pos, dim, theta):
    inv = theta ** (-jnp.arange(0, dim, 2, dtype=F32) / dim)
    ang = pos.astype(F32)[:, None] * inv[None, :]
    return jnp.cos(ang), jnp.sin(ang)


def _position_table(seq):
    rows = seq // GRID_W
    t_row = jnp.broadcast_to(jnp.arange(rows)[:, None], (rows, GRID_W)).reshape(-1)
    t_col = jnp.broadcast_to(jnp.arange(GRID_W)[None, :], (rows, GRID_W)).reshape(-1)
    row_cos, row_sin = _rope_table(t_row, HEAD_DIM // 2, AXIAL_THETA)
    col_cos, col_sin = _rope_table(t_col, HEAD_DIM // 2, AXIAL_THETA)
    cos, sin = _rope_table(jnp.arange(seq), ROPE_DIM, ROPE_THETA)
    return jnp.concatenate([row_cos, row_sin, col_cos, col_sin, cos, sin], axis=1).T


def _rms(x, g):
    return x * lax.rsqrt(jnp.mean(x * x, axis=-1, keepdims=True) + EPS) * g


def _axial_rope_t(t, tab):
    rc, rs, cc, cs = tab[0:16][None], tab[16:32][None], tab[32:48][None], tab[48:64][None]
    x1, x2, x3, x4 = t[:, 0:16], t[:, 16:32], t[:, 32:48], t[:, 48:64]
    return jnp.concatenate(
        [x1 * rc - x2 * rs, x2 * rc + x1 * rs, x3 * cc - x4 * cs, x4 * cc + x3 * cs], axis=1)


def _partial_rope_t(t, tab):
    c, s = tab[64:72][None], tab[72:80][None]
    x1, x2 = t[:, 0:8], t[:, 8:16]
    return jnp.concatenate([x1 * c - x2 * s, x2 * c + x1 * s, t[:, 16:]], axis=1)


def _head_rms_t(t, g):
    r = lax.rsqrt(jnp.mean(t * t, axis=1, keepdims=True) + EPS)
    return t * r * g[None]


def _unit_row_block(heads, rows, n):
    return (lax.broadcasted_iota(jnp.int32, (heads, rows, n), 1) == 0).astype(F32)


def _with_ones_column(t):
    heads, d, n = t.shape
    aug = jnp.concatenate([t, _unit_row_block(heads, K_LANES - d, n)], axis=1)
    return aug.reshape(heads * K_LANES, n)


def _with_ones_rows(t):
    heads, dv, n = t.shape
    aug = jnp.concatenate([t, _unit_row_block(heads, SUM_ROWS, n)], axis=1)
    return aug.reshape(heads * (dv + SUM_ROWS), n)


def _proj_t(wt_ref, lo, hi, h):
    return lax.dot_general(wt_ref[lo:hi, :], h, (((1,), (1,)), ((), ())),
                           preferred_element_type=F32)


def _even_in_kernel(x_ref, g_ref, wt_ref, tab_ref, qkg_ref,
                    qa_ref, ka_ref, va_ref, qb_ref, kb_ref, vb_ref):
    n = x_ref.shape[0]
    h = _rms(x_ref[...], g_ref[...]).astype(BF16)
    tab = tab_ref[...]
    o = 0
    t = _proj_t(wt_ref, o, o + A_Q, h).reshape(A_HEADS, HEAD_DIM, n)
    t = _axial_rope_t(_head_rms_t(t, qkg_ref[0]), tab) * Q_SCALE
    qa_ref[...] = t.reshape(A_Q, n).astype(BF16)
    o += A_Q
    t = _proj_t(wt_ref, o, o + A_KV, h).reshape(A_KV_HEADS, HEAD_DIM, n)
    t = _axial_rope_t(_head_rms_t(t, qkg_ref[1]), tab)
    ka_ref[...] = _with_ones_column(t).T.astype(BF16)
    o += A_KV
    t = _proj_t(wt_ref, o, o + A_KV, h).reshape(A_KV_HEADS, HEAD_DIM, n)
    va_ref[...] = _with_ones_rows(t).astype(BF16)
    o += A_KV
    t = _proj_t(wt_ref, o, o + B_QK, h).reshape(2 * B_HEADS, HEAD_DIM, n)
    qb_ref[...] = (_partial_rope_t(t, tab) * Q_SCALE).reshape(B_QK, n).astype(BF16)
    o += B_QK
    t = _proj_t(wt_ref, o, o + B_QK, h).reshape(2 * B_HEADS, HEAD_DIM, n)
    kb_ref[...] = _with_ones_column(_partial_rope_t(t, tab)).T.astype(BF16)
    o += B_QK
    t = _proj_t(wt_ref, o, o + B_V, h).reshape(B_HEADS, B_V_DIM, n)
    vb_ref[...] = _with_ones_rows(t).astype(BF16)


def _odd_in_kernel(x_ref, g_ref, wt_ref, tab_ref, q_ref, k_ref, v_ref):
    n = x_ref.shape[0]
    h = _rms(x_ref[...], g_ref[...]).astype(BF16)
    tab = tab_ref[...]
    t = _proj_t(wt_ref, 0, C_Q, h).reshape(C_HEADS, HEAD_DIM, n)
    q_ref[...] = (_partial_rope_t(t, tab) * Q_SCALE).reshape(C_Q, n).astype(BF16)
    t = _proj_t(wt_ref, C_Q, C_Q + C_KV, h).reshape(C_KV_HEADS, HEAD_DIM, n)
    k_ref[...] = _with_ones_column(_partial_rope_t(t, tab)).T.astype(BF16)
    t = _proj_t(wt_ref, C_Q + C_KV, ODD_IN, h).reshape(C_KV_HEADS, HEAD_DIM, n)
    v_ref[...] = _with_ones_rows(t).astype(BF16)


def _resident(shape):
    nd = len(shape)
    return pl.BlockSpec(shape, lambda *_: (0,) * nd, pipeline_mode=pl.Buffered(1))


def _token_params():
    return pltpu.CompilerParams(dimension_semantics=("parallel",), vmem_limit_bytes=VMEM_LIMIT)


def _input_projection(x, gain, wt, tab, pos_tile, qk_gain=None):
    tokens = x.shape[0]
    tm = TOKEN_TILE
    grid = (tokens // tm,)
    in_specs = [
        pl.BlockSpec((tm, D_MODEL), lambda i: (i, 0)),
        _resident((1, D_MODEL)),
        _resident(wt.shape),
        pl.BlockSpec((_TAB_ROWS, tm), lambda i: (0, pos_tile(i))),
    ]
    args = [x, gain.reshape(1, D_MODEL), wt, tab]

    def feat(rows):
        return (jax.ShapeDtypeStruct((rows, tokens), BF16), pl.BlockSpec((rows, tm), lambda i: (0, i)))

    def tokm(cols):
        return (jax.ShapeDtypeStruct((tokens, cols), BF16), pl.BlockSpec((tm, cols), lambda i: (i, 0)))

    if qk_gain is not None:
        in_specs.append(_resident((2, HEAD_DIM, 1)))
        args.append(qk_gain.reshape(2, HEAD_DIM, 1))
        outs = [feat(A_Q), tokm(A_KV_HEADS * K_LANES), feat(A_KV_HEADS * (HEAD_DIM + SUM_ROWS)),
                feat(B_QK), tokm(2 * B_HEADS * K_LANES), feat(B_HEADS * (B_V_DIM + SUM_ROWS))]
        body, name = _even_in_kernel, "even_in_proj"
    else:
        outs = [feat(C_Q), tokm(C_KV_HEADS * K_LANES), feat(C_KV_HEADS * (HEAD_DIM + SUM_ROWS))]
        body, name = _odd_in_kernel, "odd_in_proj"
    return pl.pallas_call(
        body, grid=grid, in_specs=in_specs,
        out_specs=[o[1] for o in outs], out_shape=[o[0] for o in outs],
        compiler_params=_token_params(), name=name)(*args)


def _flash_t(k_refs, vt_ref, qaug_ref, s_a, s_b, acc_ref, key_chunk):
    n = qaug_ref.shape[1]
    width = n // len(k_refs)
    nc = k_refs[0].shape[0] // key_chunk
    assert nc >= 2 and STEPS_PER_ITER % 2 == 0
    ref_rows = lax.broadcasted_iota(jnp.int32, (SUM_ROWS, n), 0) == 0

    def bf16_round(x):
        return x.astype(BF16).astype(F32)

    def set_reference(ref):
        qaug_ref[REF_ROW:REF_ROW + SUM_ROWS, :] = jnp.where(ref_rows, -ref, 0.0).astype(BF16)

    def tile_scores(t, off, rows, s_ref):
        lanes = slice(t * LANE_TILE, (t + 1) * LANE_TILE)
        k_ref = k_refs[t * LANE_TILE // width]
        s = jnp.dot(k_ref[pl.ds(off, rows), :], qaug_ref[:, lanes], preferred_element_type=F32)
        if s_ref is not None:
            s_ref[:, lanes] = s.astype(BF16)
        return jnp.max(s, axis=0, keepdims=True)

    def tile_accumulate(t, off, s_ref, alpha):
        lanes = slice(t * LANE_TILE, (t + 1) * LANE_TILE)
        p = jnp.exp2(s_ref[:, lanes].astype(F32)).astype(BF16)
        pv = jnp.dot(vt_ref[:, pl.ds(off, key_chunk)], p, preferred_element_type=F32)
        acc_ref[:, lanes] = alpha[:, lanes] * acc_ref[:, lanes] + pv

    def step(c_next, s_next, c_cur, s_cur, alpha):
        maxes = []
        for t in range(n // LANE_TILE):
            if c_next is not None:
                maxes.append(tile_scores(t, pl.multiple_of(c_next * key_chunk, key_chunk), key_chunk, s_next))
            if c_cur is not None:
                tile_accumulate(t, pl.multiple_of(c_cur * key_chunk, key_chunk), s_cur, alpha)
        return jnp.concatenate(maxes, axis=1) if maxes else None

    def sweep(top0):
        bufs = (s_a, s_b)
        acc_ref[...] = jnp.zeros(acc_ref.shape, F32)
        ref0 = bf16_round(top0)
        set_reference(ref0)
        mc0 = step(0, s_a, None, None, None)

        def overlapped_step(k, parity, state):
            ref_pp, ref_p, mc_p, top, over = state
            ref_k = bf16_round(top)
            set_reference(ref_k)
            mc_k = step(k, bufs[parity], k - 1, bufs[1 - parity], jnp.exp2(ref_pp - ref_p))
            return ref_p, ref_k, mc_k, jnp.maximum(top, ref_p + mc_p), jnp.maximum(over, mc_p)

        def body(i, state):
            for u in range(STEPS_PER_ITER):
                state = overlapped_step(STEPS_PER_ITER * i + 1 + u, (1 + u) % 2, state)
            return state

        state = (ref0, ref0, mc0, top0, jnp.zeros((1, n), F32))
        iters = (nc - 1) // STEPS_PER_ITER
        state = lax.fori_loop(0, iters, body, state)
        for k in range(iters * STEPS_PER_ITER + 1, nc):
            state = overlapped_step(k, k % 2, state)
        ref_pp, ref_p, mc_p, top, over = state
        step(None, None, nc - 1, bufs[(nc - 1) % 2], jnp.exp2(ref_pp - ref_p))
        return jnp.maximum(top, ref_p + mc_p), jnp.maximum(over, mc_p)

    set_reference(jnp.zeros((1, n), F32))
    first = jnp.concatenate([tile_scores(t, 0, K_LANES, None) for t in range(n // LANE_TILE)], axis=1)
    top, over = sweep(first)

    @pl.when(jnp.max(over) > OVERFLOW_GUARD)
    def _():
        sweep(top)


def _finish(acc_ref, dv):
    acc = acc_ref[...]
    return acc[:dv] / acc[dv:dv + 1]


def _heads_to_token_major(o, heads, tq):
    stacked = jnp.concatenate([o[:, g * tq:(g + 1) * tq] for g in range(heads)], axis=0)
    return stacked.T


def _fill_queries(qaug_ref, q_ref, heads, tq):
    n = qaug_ref.shape[1]
    for g in range(heads):
        qaug_ref[0:HEAD_DIM, g * tq:(g + 1) * tq] = q_ref[g * HEAD_DIM:(g + 1) * HEAD_DIM, :]
    qaug_ref[REF_ROW + SUM_ROWS:, :] = jnp.zeros((K_LANES - REF_ROW - SUM_ROWS, n), BF16)


def _attn_a_kernel(q_ref, k_ref, vt_ref, o_ref, qaug_ref, s_a, s_b, acc_ref, *, key_chunk):
    tq = q_ref.shape[1]
    group = A_HEADS // A_KV_HEADS
    _fill_queries(qaug_ref, q_ref, group, tq)
    _flash_t([k_ref], vt_ref, qaug_ref, s_a, s_b, acc_ref, key_chunk)
    o_ref[...] = _heads_to_token_major(_finish(acc_ref, HEAD_DIM), group, tq).astype(BF16)


def _attn_b_kernel(q_ref, k0_ref, k1_ref, vt_ref, lam_ref, o_ref, qaug_ref, s_a, s_b, acc_ref, *,
                   key_chunk, lam_init):
    tq = q_ref.shape[1]
    _fill_queries(qaug_ref, q_ref, 2, tq)
    _flash_t([k0_ref, k1_ref], vt_ref, qaug_ref, s_a, s_b, acc_ref, key_chunk)
    lf = lam_ref[...]
    lam = (jnp.exp(jnp.sum(lf[0:1] * lf[1:2], axis=1, keepdims=True))
           - jnp.exp(jnp.sum(lf[2:3] * lf[3:4], axis=1, keepdims=True)) + lam_init)
    o = _finish(acc_ref, B_V_DIM)
    o = o[:, :tq] - lam * o[:, tq:]
    o = o * lax.rsqrt(jnp.mean(o * o, axis=0, keepdims=True) + EPS) * (1.0 - lam_init)
    o_ref[...] = o.T.astype(BF16)


def _attn_c_kernel(sink_ref, q_ref, k_ref, vt_ref, o_ref, qaug_a, qaug_b, s_a, s_b, bias_ref):
    group = C_HEADS // C_KV_HEADS
    seq = k_ref.shape[0]
    tile = q_ref.shape[1] // WINDOW_TILES
    span = tile + 2 * WINDOW
    j = pl.program_id(1)
    q_base = pl.program_id(2) * q_ref.shape[1]
    rel0 = (lax.broadcasted_iota(jnp.int32, (span, tile), 0)
            - lax.broadcasted_iota(jnp.int32, (span, tile), 1))
    sinks = [jnp.full((1, tile), sink_ref[j * group + g] * LOG2E, F32) for g in range(group)]
    for qaug in (qaug_a, qaug_b):
        qaug[HEAD_DIM:, :] = jnp.zeros((K_LANES - HEAD_DIM, group * tile), BF16)

    def window(t):
        q0 = q_base + t * tile
        return q0, pl.multiple_of(jnp.clip(q0 - WINDOW, 0, seq - span), WINDOW)

    def tile_scores(t, g, qaug, s_ref):
        _, start = window(t)
        lanes = slice(g * tile, (g + 1) * tile)
        qaug[0:HEAD_DIM, lanes] = q_ref[g * HEAD_DIM:(g + 1) * HEAD_DIM, t * tile:(t + 1) * tile]
        s = jnp.dot(k_ref[pl.ds(start, span), :], qaug[:, lanes], preferred_element_type=F32)
        s = s + bias_ref[...]
        s_ref[:, lanes] = s
        return jnp.maximum(jnp.max(s, axis=0, keepdims=True), sinks[g])

    def tile_finish(t, g, s_ref, m):
        _, start = window(t)
        lanes = slice(g * tile, (g + 1) * tile)
        p = jnp.exp2(s_ref[:, lanes] - m).astype(BF16)
        acc = jnp.dot(vt_ref[:, pl.ds(start, span)], p, preferred_element_type=F32)
        l = acc[HEAD_DIM:HEAD_DIM + 1] + jnp.exp2(sinks[g] - m)
        return acc[:HEAD_DIM] / l

    bufs = ((qaug_a, s_a), (qaug_b, s_b))
    maxes = None
    for t in range(WINDOW_TILES + 1):
        new_maxes, outs = [], []
        if t < WINDOW_TILES:
            q0, start = window(t)
            in_band = jnp.abs(rel0 + (start - q0)) <= WINDOW
            bias_ref[...] = jnp.where(in_band, 0.0, -jnp.inf)
        for g in range(group):
            if t < WINDOW_TILES:
                new_maxes.append(tile_scores(t, g, *bufs[t % 2]))
            if t > 0:
                outs.append(tile_finish(t - 1, g, bufs[(t - 1) % 2][1], maxes[g]))
        if t > 0:
            o_ref[(t - 1) * tile:t * tile, :] = jnp.concatenate(outs, axis=0).T.astype(BF16)
        maxes = new_maxes


def _attn_params():
    return pltpu.CompilerParams(dimension_semantics=("parallel", "parallel", "arbitrary"),
                                vmem_limit_bytes=VMEM_LIMIT)


def _flash_scratch(key_chunk, n, dv):
    return [pltpu.VMEM((K_LANES, n), BF16), pltpu.VMEM((key_chunk, n), BF16),
            pltpu.VMEM((key_chunk, n), BF16), pltpu.VMEM((dv + SUM_ROWS, n), F32)]


def _attention_a(qt, k, vt, batch, seq):
    tq = 2 * Q_TILE
    nq = seq // tq
    group = A_HEADS // A_KV_HEADS
    width = group * HEAD_DIM
    key_chunk = min(KEY_CHUNK, seq // 2)
    return pl.pallas_call(
        functools.partial(_attn_a_kernel, key_chunk=key_chunk),
        grid=(batch, A_KV_HEADS, nq),
        scratch_shapes=_flash_scratch(key_chunk, group * tq, HEAD_DIM),
        in_specs=[
            pl.BlockSpec((width, tq), lambda b, j, i: (j, b * nq + i)),
            pl.BlockSpec((seq, K_LANES), lambda b, j, i: (b, j)),
            pl.BlockSpec((HEAD_DIM + SUM_ROWS, seq), lambda b, j, i: (j, b)),
        ],
        out_specs=pl.BlockSpec((tq, width), lambda b, j, i: (b * nq + i, j)),
        out_shape=jax.ShapeDtypeStruct((batch * seq, A_Q), BF16),
        compiler_params=_attn_params(), name="attn_axial")(qt, k, vt)


def _attention_b(qt, k, vt, lam_w, lam_init, batch, seq):
    tq = 4 * Q_TILE
    nq = seq // tq
    key_chunk = min(KEY_CHUNK, seq // 2)
    return pl.pallas_call(
        functools.partial(_attn_b_kernel, key_chunk=key_chunk, lam_init=lam_init),
        grid=(batch, B_HEADS, nq),
        scratch_shapes=_flash_scratch(key_chunk, 2 * tq, B_V_DIM),
        in_specs=[
            pl.BlockSpec((2 * HEAD_DIM, tq), lambda b, j, i: (j, b * nq + i)),
            pl.BlockSpec((seq, K_LANES), lambda b, j, i: (b, 2 * j)),
            pl.BlockSpec((seq, K_LANES), lambda b, j, i: (b, 2 * j + 1)),
            pl.BlockSpec((B_V_DIM + SUM_ROWS, seq), lambda b, j, i: (j, b)),
            pl.BlockSpec((4, HEAD_DIM), lambda b, j, i: (0, 0)),
        ],
        out_specs=pl.BlockSpec((tq, B_V_DIM), lambda b, j, i: (b * nq + i, j)),
        out_shape=jax.ShapeDtypeStruct((batch * seq, B_V), BF16),
        compiler_params=_attn_params(), name="attn_diff")(qt, k, k, vt, lam_w)


def _attention_c(qt, k, vt, sink, batch, seq):
    tq = WINDOW_TILES * Q_TILE
    nq = seq // tq
    group = C_HEADS // C_KV_HEADS
    width = group * HEAD_DIM
    span = Q_TILE + 2 * WINDOW
    return pl.pallas_call(
        _attn_c_kernel,
        grid=(batch, C_KV_HEADS, nq),
        scratch_shapes=[pltpu.VMEM((K_LANES, group * Q_TILE), BF16), pltpu.VMEM((K_LANES, group * Q_TILE), BF16),
                        pltpu.VMEM((span, group * Q_TILE), F32), pltpu.VMEM((span, group * Q_TILE), F32),
                        pltpu.VMEM((span, Q_TILE), F32)],
        in_specs=[
            pl.BlockSpec(memory_space=pltpu.SMEM),
            pl.BlockSpec((width, tq), lambda b, j, i: (j, b * nq + i)),
            pl.BlockSpec((seq, K_LANES), lambda b, j, i: (b, j)),
            pl.BlockSpec((HEAD_DIM + SUM_ROWS, seq), lambda b, j, i: (j, b)),
        ],
        out_specs=pl.BlockSpec((tq, width), lambda b, j, i: (b * nq + i, j)),
        out_shape=jax.ShapeDtypeStruct((batch * seq, C_Q), BF16),
        compiler_params=_attn_params(), name="attn_window")(sink, qt, k, vt)


def _post_kernel(lo_ref, hi_ref, x_ref, wo_ref, gmp_ref, gfp_ref, wgu_ref, wd_ref, gfo_ref, y_ref):
    half = lo_ref.shape[1]
    mix = (jnp.dot(lo_ref[...], wo_ref[0:half, :], preferred_element_type=F32)
           + jnp.dot(hi_ref[...], wo_ref[half:, :], preferred_element_type=F32))
    x = x_ref[...] + _rms(mix, gmp_ref[...])
    h = _rms(x, gfp_ref[...]).astype(BF16)
    acc = jnp.zeros(x.shape, F32)
    for c in range(FFN_HIDDEN // FFN_CHUNK):
        lo = c * FFN_CHUNK
        g = jnp.dot(h, wgu_ref[:, lo:lo + FFN_CHUNK], preferred_element_type=F32)
        u = jnp.dot(h, wgu_ref[:, FFN_HIDDEN + lo:FFN_HIDDEN + lo + FFN_CHUNK], preferred_element_type=F32)
        a = (g * (1.0 / (1.0 + jnp.exp(-g))) * u).astype(BF16)
        acc = acc + jnp.dot(a, wd_ref[lo:lo + FFN_CHUNK, :], preferred_element_type=F32)
    y_ref[...] = x + _rms(acc, gfo_ref[...])


def _post_attention(o_lo, o_hi, lo_blk, hi_blk, x, w_out, g_mix_post, g_ffn_pre, w_gu, w_down, g_ffn_post):
    tokens = x.shape[0]
    tm = TOKEN_TILE
    half = D_MODEL // 2
    row = lambda v: v.reshape(1, D_MODEL)
    return pl.pallas_call(
        _post_kernel, grid=(tokens // tm,),
        in_specs=[
            pl.BlockSpec((tm, half), lambda i: (i, lo_blk)),
            pl.BlockSpec((tm, half), lambda i: (i, hi_blk)),
            pl.BlockSpec((tm, D_MODEL), lambda i: (i, 0)),
            _resident(w_out.shape), _resident((1, D_MODEL)), _resident((1, D_MODEL)),
            _resident(w_gu.shape), _resident(w_down.shape), _resident((1, D_MODEL)),
        ],
        out_specs=pl.BlockSpec((tm, D_MODEL), lambda i: (i, 0)),
        out_shape=jax.ShapeDtypeStruct((tokens, D_MODEL), F32),
        compiler_params=_token_params(), name="out_proj_ffn")(
            o_lo, o_hi, x, w_out, row(g_mix_post), row(g_ffn_pre), w_gu, w_down, row(g_ffn_post))


def _trunk(x3, weights, tab):
    batch, seq, _ = x3.shape
    assert seq % TOKEN_TILE == 0 and seq % (4 * Q_TILE) == 0 and seq >= Q_TILE + 2 * WINDOW
    x = x3.reshape(batch * seq, D_MODEL)
    tiles_per_seq = seq // TOKEN_TILE
    pos_tile = lambda i: i % tiles_per_seq
    for l, w in enumerate(weights):
        if l % 2 == 0:
            qa, ka, va, qb, kb, vb = _input_projection(x, w["g_mix_pre"], w["wt_in"], tab, pos_tile, w["qk_gain"])
            o_lo = _attention_a(qa, ka, va, batch, seq)
            o_hi = _attention_b(qb, kb, vb, w["lam_w"], w["lam_init"], batch, seq)
            lo_blk, hi_blk = 0, 0
        else:
            q, k, v = _input_projection(x, w["g_mix_pre"], w["wt_in"], tab, pos_tile)
            o_lo = o_hi = _attention_c(q, k, v, w["sink"], batch, seq)
            lo_blk, hi_blk = 0, 1
        x = _post_attention(o_lo, o_hi, lo_blk, hi_blk, x, w["w_out"], w["g_mix_post"], w["g_ffn_pre"],
                            w["w_gu"], w["w_down"], w["g_ffn_post"])
    return x.reshape(batch, seq, D_MODEL)


def kernel(x_prompt, x_sample, w_in_even, w_out_even, qk_norm_a, diff_lambda, w_in_odd, w_out_odd,
           sink_c, w_gate_up, w_down, norm_mix_pre, norm_mix_post, norm_ffn_pre, norm_ffn_post):
    weights = []
    for l in range(w_gate_up.shape[0]):
        e = l // 2
        w = dict(g_mix_pre=norm_mix_pre[l], g_mix_post=norm_mix_post[l], g_ffn_pre=norm_ffn_pre[l],
                 g_ffn_post=norm_ffn_post[l], w_gu=w_gate_up[l].astype(BF16), w_down=w_down[l].astype(BF16))
        if l % 2 == 0:
            w.update(wt_in=w_in_even[e].T.astype(BF16), w_out=w_out_even[e].astype(BF16),
                     qk_gain=qk_norm_a[e], lam_w=diff_lambda[e], lam_init=0.8 - 0.6 * math.exp(-0.3 * l))
        else:
            w.update(wt_in=w_in_odd[e].T.astype(BF16), w_out=w_out_odd[e].astype(BF16), sink=sink_c[e])
        weights.append(w)
    tab = _position_table(max(x_prompt.shape[1], x_sample.shape[1]))
    return (_trunk(x_prompt, weights, tab), _trunk(x_sample, weights, tab))
```

```python
import functools
import math

import jax
import jax.numpy as jnp
from jax import lax
from jax.experimental import pallas as pl
from jax.experimental.pallas import tpu as pltpu

F32 = jnp.float32
BF16 = jnp.bfloat16

D_MODEL = 1024
HEAD_DIM = 64
GRID_W = 64
EPS = 1e-6
A_HEADS = 8
A_KV_HEADS = 2
AXIAL_THETA = 10000.0
B_HEADS = 4
B_V_DIM = 2 * HEAD_DIM
C_HEADS = 16
C_KV_HEADS = 4
WINDOW = 128
ROPE_THETA = 500000.0
ROPE_DIM = HEAD_DIM // 4
FFN_HIDDEN = 2816

A_Q = A_HEADS * HEAD_DIM
A_KV = A_KV_HEADS * HEAD_DIM
B_QK = B_HEADS * 2 * HEAD_DIM
B_V = B_HEADS * B_V_DIM
EVEN_IN = A_Q + 2 * A_KV + 2 * B_QK + B_V
C_Q = C_HEADS * HEAD_DIM
C_KV = C_KV_HEADS * HEAD_DIM
ODD_IN = C_Q + 2 * C_KV

LOG2E = 1.4426950408889634
Q_SCALE = HEAD_DIM ** -0.5 * LOG2E

TOKEN_TILE = 512
Q_TILE = 256
LANE_TILE = 256
WINDOW_TILES = 4
KEY_CHUNK = 256
STEPS_PER_ITER = 8
FFN_CHUNK = 256
VMEM_LIMIT = 56 * 1024 * 1024
K_LANES = 128
SUM_ROWS = 16
REF_ROW = HEAD_DIM
OVERFLOW_GUARD = 100.0

_TAB_ROWS = 80


def _rope_table(pos, dim, theta):
    inv = theta ** (-jnp.arange(0, dim, 2, dtype=F32) / dim)
    ang = pos.astype(F32)[:, None] * inv[None, :]
    return jnp.cos(ang), jnp.sin(ang)


def _position_table(seq):
    rows = seq // GRID_W
    t_row = jnp.broadcast_to(jnp.arange(rows)[:, None], (rows, GRID_W)).reshape(-1)
    t_col = jnp.broadcast_to(jnp.arange(GRID_W)[None, :], (rows, GRID_W)).reshape(-1)
    row_cos, row_sin = _rope_table(t_row, HEAD_DIM // 2, AXIAL_THETA)
    col_cos, col_sin = _rope_table(t_col, HEAD_DIM // 2, AXIAL_THETA)
    cos, sin = _rope_table(jnp.arange(seq), ROPE_DIM, ROPE_THETA)
    return jnp.concatenate([row_cos, row_sin, col_cos, col_sin, cos, sin], axis=1).T


def _rms(x, g):
    return x * lax.rsqrt(jnp.mean(x * x, axis=-1, keepdims=True) + EPS) * g


def _axial_rope_t(t, tab):
    rc, rs, cc, cs = tab[0:16][None], tab[16:32][None], tab[32:48][None], tab[48:64][None]
    x1, x2, x3, x4 = t[:, 0:16], t[:, 16:32], t[:, 32:48], t[:, 48:64]
    return jnp.concatenate(
        [x1 * rc - x2 * rs, x2 * rc + x1 * rs, x3 * cc - x4 * cs, x4 * cc + x3 * cs], axis=1)


def _partial_rope_t(t, tab):
    c, s = tab[64:72][None], tab[72:80][None]
    x1, x2 = t[:, 0:8], t[:, 8:16]
    return jnp.concatenate([x1 * c - x2 * s, x2 * c + x1 * s, t[:, 16:]], axis=1)


def _head_rms_t(t, g):
    r = lax.rsqrt(jnp.mean(t * t, axis=1, keepdims=True) + EPS)
    return t * r * g[None]


def _unit_row_block(heads, rows, n):
    return (lax.broadcasted_iota(jnp.int32, (heads, rows, n), 1) == 0).astype(F32)


def _with_ones_column(t):
    heads, d, n = t.shape
    aug = jnp.concatenate([t, _unit_row_block(heads, K_LANES - d, n)], axis=1)
    return aug.reshape(heads * K_LANES, n)


def _with_ones_rows(t):
    heads, dv, n = t.shape
    aug = jnp.concatenate([t, _unit_row_block(heads, SUM_ROWS, n)], axis=1)
    return aug.reshape(heads * (dv + SUM_ROWS), n)


def _proj_t(wt_ref, lo, hi, h):
    return lax.dot_general(wt_ref[lo:hi, :], h, (((1,), (1,)), ((), ())),
                           preferred_element_type=F32)


def _even_in_kernel(x_ref, g_ref, wt_ref, tab_ref, qkg_ref,
                    qa_ref, ka_ref, va_ref, qb_ref, kb_ref, vb_ref):
    n = x_ref.shape[0]
    h = _rms(x_ref[...], g_ref[...]).astype(BF16)
    tab = tab_ref[...]
    o = 0
    t = _proj_t(wt_ref, o, o + A_Q, h).reshape(A_HEADS, HEAD_DIM, n)
    t = _axial_rope_t(_head_rms_t(t, qkg_ref[0]), tab) * Q_SCALE
    qa_ref[...] = t.reshape(A_Q, n).astype(BF16)
    o += A_Q
    t = _proj_t(wt_ref, o, o + A_KV, h).reshape(A_KV_HEADS, HEAD_DIM, n)
    t = _axial_rope_t(_head_rms_t(t, qkg_ref[1]), tab)
    ka_ref[...] = _with_ones_column(t).T.astype(BF16)
    o += A_KV
    t = _proj_t(wt_ref, o, o + A_KV, h).reshape(A_KV_HEADS, HEAD_DIM, n)
    va_ref[...] = _with_ones_rows(t).astype(BF16)
    o += A_KV
    t = _proj_t(wt_ref, o, o + B_QK, h).reshape(2 * B_HEADS, HEAD_DIM, n)
    qb_ref[...] = (_partial_rope_t(t, tab) * Q_SCALE).reshape(B_QK, n).astype(BF16)
    o += B_QK
    t = _proj_t(wt_ref, o, o + B_QK, h).reshape(2 * B_HEADS, HEAD_DIM, n)
    kb_ref[...] = _with_ones_column(_partial_rope_t(t, tab)).T.astype(BF16)
    o += B_QK
    t = _proj_t(wt_ref, o, o + B_V, h).reshape(B_HEADS, B_V_DIM, n)
    vb_ref[...] = _with_ones_rows(t).astype(BF16)


def _odd_in_kernel(x_ref, g_ref, wt_ref, tab_ref, q_ref, k_ref, v_ref):
    n = x_ref.shape[0]
    h = _rms(x_ref[...], g_ref[...]).astype(BF16)
    tab = tab_ref[...]
    t = _proj_t(wt_ref, 0, C_Q, h).reshape(C_HEADS, HEAD_DIM, n)
    q_ref[...] = (_partial_rope_t(t, tab) * Q_SCALE).reshape(C_Q, n).astype(BF16)
    t = _proj_t(wt_ref, C_Q, C_Q + C_KV, h).reshape(C_KV_HEADS, HEAD_DIM, n)
    k_ref[...] = _with_ones_column(_partial_rope_t(t, tab)).T.astype(BF16)
    t = _proj_t(wt_ref, C_Q + C_KV, ODD_IN, h).reshape(C_KV_HEADS, HEAD_DIM, n)
    v_ref[...] = _with_ones_rows(t).astype(BF16)


def _resident(shape):
    nd = len(shape)
    return pl.BlockSpec(shape, lambda *_: (0,) * nd, pipeline_mode=pl.Buffered(1))


def _token_params():
    return pltpu.CompilerParams(dimension_semantics=("parallel",), vmem_limit_bytes=VMEM_LIMIT)


def _input_projection(x, gain, wt, tab, pos_tile, qk_gain=None):
    tokens = x.shape[0]
    tm = TOKEN_TILE
    grid = (tokens // tm,)
    in_specs = [
        pl.BlockSpec((tm, D_MODEL), lambda i: (i, 0)),
        _resident((1, D_MODEL)),
        _resident(wt.shape),
        pl.BlockSpec((_TAB_ROWS, tm), lambda i: (0, pos_tile(i))),
    ]
    args = [x, gain.reshape(1, D_MODEL), wt, tab]

    def feat(rows):
        return (jax.ShapeDtypeStruct((rows, tokens), BF16), pl.BlockSpec((rows, tm), lambda i: (0, i)))

    def tokm(cols):
        return (jax.ShapeDtypeStruct((tokens, cols), BF16), pl.BlockSpec((tm, cols), lambda i: (i, 0)))

    if qk_gain is not None:
        in_specs.append(_resident((2, HEAD_DIM, 1)))
        args.append(qk_gain.reshape(2, HEAD_DIM, 1))
        outs = [feat(A_Q), tokm(A_KV_HEADS * K_LANES), feat(A_KV_HEADS * (HEAD_DIM + SUM_ROWS)),
                feat(B_QK), tokm(2 * B_HEADS * K_LANES), feat(B_HEADS * (B_V_DIM + SUM_ROWS))]
        body, name = _even_in_kernel, "even_in_proj"
    else:
        outs = [feat(C_Q), tokm(C_KV_HEADS * K_LANES), feat(C_KV_HEADS * (HEAD_DIM + SUM_ROWS))]
        body, name = _odd_in_kernel, "odd_in_proj"
    return pl.pallas_call(
        body, grid=grid, in_specs=in_specs,
        out_specs=[o[1] for o in outs], out_shape=[o[0] for o in outs],
        compiler_params=_token_params(), name=name)(*args)


def _flash_t(k_refs, vt_ref, qaug_ref, s_a, s_b, acc_ref, key_chunk):
    n = qaug_ref.shape[1]
    width = n // len(k_refs)
    nc = k_refs[0].shape[0] // key_chunk
    assert nc >= 2 and STEPS_PER_ITER % 2 == 0 and STEPS_PER_ITER >= 2
    ref_rows = lax.broadcasted_iota(jnp.int32, (SUM_ROWS, n), 0) == 0

    def bf16_round(x):
        return x.astype(BF16).astype(F32)

    def set_reference(ref):
        qaug_ref[REF_ROW:REF_ROW + SUM_ROWS, :] = jnp.where(ref_rows, -ref, 0.0).astype(BF16)

    def tile_scores(t, off, rows, s_ref):
        lanes = slice(t * LANE_TILE, (t + 1) * LANE_TILE)
        k_ref = k_refs[t * LANE_TILE // width]
        s = jnp.dot(k_ref[pl.ds(off, rows), :], qaug_ref[:, lanes], preferred_element_type=F32)
        if s_ref is not None:
            s_ref[:, lanes] = s.astype(BF16)
        return jnp.max(s, axis=0, keepdims=True)

    def tile_accumulate(t, off, s_ref, alpha):
        lanes = slice(t * LANE_TILE, (t + 1) * LANE_TILE)
        p = jnp.exp2(s_ref[:, lanes])
        pv = jnp.dot(vt_ref[:, pl.ds(off, key_chunk)], p, preferred_element_type=F32)
        acc = acc_ref[:, lanes]
        acc_ref[:, lanes] = (acc if alpha is None else alpha[:, lanes] * acc) + pv

    def step(c_next, s_next, c_cur, s_cur, alpha):
        maxes = []
        for t in range(n // LANE_TILE):
            if c_next is not None:
                maxes.append(tile_scores(t, pl.multiple_of(c_next * key_chunk, key_chunk), key_chunk, s_next))
            if c_cur is not None:
                tile_accumulate(t, pl.multiple_of(c_cur * key_chunk, key_chunk), s_cur, alpha)
        return jnp.concatenate(maxes, axis=1) if maxes else None

    def sweep(top0):
        bufs = (s_a, s_b)
        acc_ref[...] = jnp.zeros(acc_ref.shape, F32)
        ref0 = bf16_round(top0)
        set_reference(ref0)
        mc0 = step(0, s_a, None, None, None)

        def group(k0, count, state, move_reference=True):
            ref_p, mc_p, top, over = state
            ref_g = ref_p
            if move_reference:
                ref_g = bf16_round(top)
                set_reference(ref_g)
            for u in range(count):
                alpha = jnp.exp2(ref_p - ref_g) if (u == 1 and move_reference) else None
                mc_k = step(k0 + u, bufs[(1 + u) % 2], k0 + u - 1, bufs[u % 2], alpha)
                top = jnp.maximum(top, (ref_p if u == 0 else ref_g) + mc_p)
                over = jnp.maximum(over, mc_p)
                mc_p = mc_k
            return ref_g, mc_p, top, over

        state = (ref0, mc0, top0, jnp.zeros((1, n), F32))
        iters = (nc - 1) // STEPS_PER_ITER
        state = lax.fori_loop(0, iters, lambda i, st: group(STEPS_PER_ITER * i + 1, STEPS_PER_ITER, st), state)
        rest = nc - 1 - iters * STEPS_PER_ITER
        if rest:
            state = group(iters * STEPS_PER_ITER + 1, rest, state, move_reference=rest > 1)
        ref_p, mc_p, top, over = state
        step(None, None, nc - 1, bufs[(nc - 1) % 2], None)
        return jnp.maximum(top, ref_p + mc_p), jnp.maximum(over, mc_p)

    set_reference(jnp.zeros((1, n), F32))
    first = jnp.concatenate([tile_scores(t, 0, K_LANES, None) for t in range(n // LANE_TILE)], axis=1)
    top, over = sweep(first)

    @pl.when(jnp.max(over) > OVERFLOW_GUARD)
    def _():
        sweep(top)


def _finish(acc_ref, dv):
    acc = acc_ref[...]
    return acc[:dv] / acc[dv:dv + 1]


def _heads_to_token_major(o, heads, tq):
    stacked = jnp.concatenate([o[:, g * tq:(g + 1) * tq] for g in range(heads)], axis=0)
    return stacked.T


def _fill_queries(qaug_ref, q_ref, heads, tq):
    n = qaug_ref.shape[1]
    for g in range(heads):
        qaug_ref[0:HEAD_DIM, g * tq:(g + 1) * tq] = q_ref[g * HEAD_DIM:(g + 1) * HEAD_DIM, :]
    qaug_ref[REF_ROW + SUM_ROWS:, :] = jnp.zeros((K_LANES - REF_ROW - SUM_ROWS, n), BF16)


def _attn_a_kernel(q_ref, k_ref, vt_ref, o_ref, qaug_ref, s_a, s_b, acc_ref, *, key_chunk):
    tq = q_ref.shape[1]
    group = A_HEADS // A_KV_HEADS
    _fill_queries(qaug_ref, q_ref, group, tq)
    _flash_t([k_ref], vt_ref, qaug_ref, s_a, s_b, acc_ref, key_chunk)
    o_ref[...] = _heads_to_token_major(_finish(acc_ref, HEAD_DIM), group, tq).astype(BF16)


def _attn_b_kernel(q_ref, k0_ref, k1_ref, vt_ref, lam_ref, o_ref, qaug_ref, s_a, s_b, acc_ref, *,
                   key_chunk, lam_init):
    tq = q_ref.shape[1]
    _fill_queries(qaug_ref, q_ref, 2, tq)
    _flash_t([k0_ref, k1_ref], vt_ref, qaug_ref, s_a, s_b, acc_ref, key_chunk)
    lf = lam_ref[...]
    lam = (jnp.exp(jnp.sum(lf[0:1] * lf[1:2], axis=1, keepdims=True))
           - jnp.exp(jnp.sum(lf[2:3] * lf[3:4], axis=1, keepdims=True)) + lam_init)
    o = _finish(acc_ref, B_V_DIM)
    o = o[:, :tq] - lam * o[:, tq:]
    o = o * lax.rsqrt(jnp.mean(o * o, axis=0, keepdims=True) + EPS) * (1.0 - lam_init)
    o_ref[...] = o.T.astype(BF16)


def _attn_c_kernel(sink_ref, q_ref, k_ref, vt_ref, o_ref, qaug_a, qaug_b, s_a, s_b, bias_ref):
    group = C_HEADS // C_KV_HEADS
    seq = k_ref.shape[0]
    tile = q_ref.shape[1] // WINDOW_TILES
    span = tile + 2 * WINDOW
    j = pl.program_id(1)
    q_base = pl.program_id(2) * q_ref.shape[1]
    rel0 = (lax.broadcasted_iota(jnp.int32, (span, tile), 0)
            - lax.broadcasted_iota(jnp.int32, (span, tile), 1))
    sinks = [jnp.full((1, tile), sink_ref[j * group + g] * LOG2E, F32) for g in range(group)]
    for qaug in (qaug_a, qaug_b):
        qaug[HEAD_DIM:, :] = jnp.zeros((K_LANES - HEAD_DIM, group * tile), BF16)

    def window(t):
        q0 = q_base + t * tile
        return q0, pl.multiple_of(jnp.clip(q0 - WINDOW, 0, seq - span), WINDOW)

    def tile_scores(t, g, qaug, s_ref):
        _, start = window(t)
        lanes = slice(g * tile, (g + 1) * tile)
        qaug[0:HEAD_DIM, lanes] = q_ref[g * HEAD_DIM:(g + 1) * HEAD_DIM, t * tile:(t + 1) * tile]
        s = jnp.dot(k_ref[pl.ds(start, span), :], qaug[:, lanes], preferred_element_type=F32)
        s = s + bias_ref[...]
        s_ref[:, lanes] = s
        return jnp.maximum(jnp.max(s, axis=0, keepdims=True), sinks[g])

    def tile_finish(t, g, s_ref, m):
        _, start = window(t)
        lanes = slice(g * tile, (g + 1) * tile)
        p = jnp.exp2(s_ref[:, lanes] - m).astype(BF16)
        acc = jnp.dot(vt_ref[:, pl.ds(start, span)], p, preferred_element_type=F32)
        l = acc[HEAD_DIM:HEAD_DIM + 1] + jnp.exp2(sinks[g] - m)
        return acc[:HEAD_DIM] / l

    bufs = ((qaug_a, s_a), (qaug_b, s_b))
    maxes = None
    for t in range(WINDOW_TILES + 1):
        new_maxes, outs = [], []
        if t < WINDOW_TILES:
            q0, start = window(t)
            in_band = jnp.abs(rel0 + (start - q0)) <= WINDOW
            bias_ref[...] = jnp.where(in_band, 0.0, -jnp.inf)
        for g in range(group):
            if t < WINDOW_TILES:
                new_maxes.append(tile_scores(t, g, *bufs[t % 2]))
            if t > 0:
                outs.append(tile_finish(t - 1, g, bufs[(t - 1) % 2][1], maxes[g]))
        if t > 0:
            o_ref[(t - 1) * tile:t * tile, :] = jnp.concatenate(outs, axis=0).T.astype(BF16)
        maxes = new_maxes


def _attn_params():
    return pltpu.CompilerParams(dimension_semantics=("parallel", "parallel", "arbitrary"),
                                vmem_limit_bytes=VMEM_LIMIT)


def _flash_scratch(key_chunk, n, dv):
    return [pltpu.VMEM((K_LANES, n), BF16), pltpu.VMEM((key_chunk, n), BF16),
            pltpu.VMEM((key_chunk, n), BF16), pltpu.VMEM((dv + SUM_ROWS, n), F32)]


def _attention_a(qt, k, vt, batch, seq):
    tq = 2 * Q_TILE
    nq = seq // tq
    group = A_HEADS // A_KV_HEADS
    width = group * HEAD_DIM
    key_chunk = min(KEY_CHUNK, seq // 2)
    return pl.pallas_call(
        functools.partial(_attn_a_kernel, key_chunk=key_chunk),
        grid=(batch, A_KV_HEADS, nq),
        scratch_shapes=_flash_scratch(key_chunk, group * tq, HEAD_DIM),
        in_specs=[
            pl.BlockSpec((width, tq), lambda b, j, i: (j, b * nq + i)),
            pl.BlockSpec((seq, K_LANES), lambda b, j, i: (b, j)),
            pl.BlockSpec((HEAD_DIM + SUM_ROWS, seq), lambda b, j, i: (j, b)),
        ],
        out_specs=pl.BlockSpec((tq, width), lambda b, j, i: (b * nq + i, j)),
        out_shape=jax.ShapeDtypeStruct((batch * seq, A_Q), BF16),
        compiler_params=_attn_params(), name="attn_axial")(qt, k, vt)


def _attention_b(qt, k, vt, lam_w, lam_init, batch, seq):
    tq = 4 * Q_TILE
    nq = seq // tq
    key_chunk = min(KEY_CHUNK, seq // 2)
    return pl.pallas_call(
        functools.partial(_attn_b_kernel, key_chunk=key_chunk, lam_init=lam_init),
        grid=(batch, B_HEADS, nq),
        scratch_shapes=_flash_scratch(key_chunk, 2 * tq, B_V_DIM),
        in_specs=[
            pl.BlockSpec((2 * HEAD_DIM, tq), lambda b, j, i: (j, b * nq + i)),
            pl.BlockSpec((seq, K_LANES), lambda b, j, i: (b, 2 * j)),
            pl.BlockSpec((seq, K_LANES), lambda b, j, i: (b, 2 * j + 1)),
            pl.BlockSpec((B_V_DIM + SUM_ROWS, seq), lambda b, j, i: (j, b)),
            pl.BlockSpec((4, HEAD_DIM), lambda b, j, i: (0, 0)),
        ],
        out_specs=pl.BlockSpec((tq, B_V_DIM), lambda b, j, i: (b * nq + i, j)),
        out_shape=jax.ShapeDtypeStruct((batch * seq, B_V), BF16),
        compiler_params=_attn_params(), name="attn_diff")(qt, k, k, vt, lam_w)


def _attention_c(qt, k, vt, sink, batch, seq):
    tq = WINDOW_TILES * Q_TILE
    nq = seq // tq
    group = C_HEADS // C_KV_HEADS
    width = group * HEAD_DIM
    span = Q_TILE + 2 * WINDOW
    return pl.pallas_call(
        _attn_c_kernel,
        grid=(batch, C_KV_HEADS, nq),
        scratch_shapes=[pltpu.VMEM((K_LANES, group * Q_TILE), BF16), pltpu.VMEM((K_LANES, group * Q_TILE), BF16),
                        pltpu.VMEM((span, group * Q_TILE), F32), pltpu.VMEM((span, group * Q_TILE), F32),
                        pltpu.VMEM((span, Q_TILE), F32)],
        in_specs=[
            pl.BlockSpec(memory_space=pltpu.SMEM),
            pl.BlockSpec((width, tq), lambda b, j, i: (j, b * nq + i)),
            pl.BlockSpec((seq, K_LANES), lambda b, j, i: (b, j)),
            pl.BlockSpec((HEAD_DIM + SUM_ROWS, seq), lambda b, j, i: (j, b)),
        ],
        out_specs=pl.BlockSpec((tq, width), lambda b, j, i: (b * nq + i, j)),
        out_shape=jax.ShapeDtypeStruct((batch * seq, C_Q), BF16),
        compiler_params=_attn_params(), name="attn_window")(sink, qt, k, vt)


def _post_kernel(lo_ref, hi_ref, x_ref, wo_ref, gmp_ref, gfp_ref, wgu_ref, wd_ref, gfo_ref, y_ref):
    half = lo_ref.shape[1]
    mix = (jnp.dot(lo_ref[...], wo_ref[0:half, :], preferred_element_type=F32)
           + jnp.dot(hi_ref[...], wo_ref[half:, :], preferred_element_type=F32))
    x = x_ref[...] + _rms(mix, gmp_ref[...])
    h = _rms(x, gfp_ref[...]).astype(BF16)
    acc = jnp.zeros(x.shape, F32)
    for c in range(FFN_HIDDEN // FFN_CHUNK):
        lo = c * FFN_CHUNK
        g = jnp.dot(h, wgu_ref[:, lo:lo + FFN_CHUNK], preferred_element_type=F32)
        u = jnp.dot(h, wgu_ref[:, FFN_HIDDEN + lo:FFN_HIDDEN + lo + FFN_CHUNK], preferred_element_type=F32)
        a = (g * (1.0 / (1.0 + jnp.exp(-g))) * u).astype(BF16)
        acc = acc + jnp.dot(a, wd_ref[lo:lo + FFN_CHUNK, :], preferred_element_type=F32)
    y_ref[...] = x + _rms(acc, gfo_ref[...])


def _post_attention(o_lo, o_hi, lo_blk, hi_blk, x, w_out, g_mix_post, g_ffn_pre, w_gu, w_down, g_ffn_post):
    tokens = x.shape[0]
    tm = TOKEN_TILE
    half = D_MODEL // 2
    row = lambda v: v.reshape(1, D_MODEL)
    return pl.pallas_call(
        _post_kernel, grid=(tokens // tm,),
        in_specs=[
            pl.BlockSpec((tm, half), lambda i: (i, lo_blk)),
            pl.BlockSpec((tm, half), lambda i: (i, hi_blk)),
            pl.BlockSpec((tm, D_MODEL), lambda i: (i, 0)),
            _resident(w_out.shape), _resident((1, D_MODEL)), _resident((1, D_MODEL)),
            _resident(w_gu.shape), _resident(w_down.shape), _resident((1, D_MODEL)),
        ],
        out_specs=pl.BlockSpec((tm, D_MODEL), lambda i: (i, 0)),
        out_shape=jax.ShapeDtypeStruct((tokens, D_MODEL), F32),
        compiler_params=_token_params(), name="out_proj_ffn")(
            o_lo, o_hi, x, w_out, row(g_mix_post), row(g_ffn_pre), w_gu, w_down, row(g_ffn_post))


def _trunk(x3, weights, tab):
    batch, seq, _ = x3.shape
    assert seq % TOKEN_TILE == 0 and seq % (4 * Q_TILE) == 0 and seq >= Q_TILE + 2 * WINDOW
    x = x3.reshape(batch * seq, D_MODEL)
    tiles_per_seq = seq // TOKEN_TILE
    pos_tile = lambda i: i % tiles_per_seq
    for l, w in enumerate(weights):
        if l % 2 == 0:
            qa, ka, va, qb, kb, vb = _input_projection(x, w["g_mix_pre"], w["wt_in"], tab, pos_tile, w["qk_gain"])
            o_lo = _attention_a(qa, ka, va, batch, seq)
            o_hi = _attention_b(qb, kb, vb, w["lam_w"], w["lam_init"], batch, seq)
            lo_blk, hi_blk = 0, 0
        else:
            q, k, v = _input_projection(x, w["g_mix_pre"], w["wt_in"], tab, pos_tile)
            o_lo = o_hi = _attention_c(q, k, v, w["sink"], batch, seq)
            lo_blk, hi_blk = 0, 1
        x = _post_attention(o_lo, o_hi, lo_blk, hi_blk, x, w["w_out"], w["g_mix_post"], w["g_ffn_pre"],
                            w["w_gu"], w["w_down"], w["g_ffn_post"])
    return x.reshape(batch, seq, D_MODEL)


def kernel(x_prompt, x_sample, w_in_even, w_out_even, qk_norm_a, diff_lambda, w_in_odd, w_out_odd,
           sink_c, w_gate_up, w_down, norm_mix_pre, norm_mix_post, norm_ffn_pre, norm_ffn_post):
    weights = []
    for l in range(w_gate_up.shape[0]):
        e = l // 2
        w = dict(g_mix_pre=norm_mix_pre[l], g_mix_post=norm_mix_post[l], g_ffn_pre=norm_ffn_pre[l],
                 g_ffn_post=norm_ffn_post[l], w_gu=w_gate_up[l].astype(BF16), w_down=w_down[l].astype(BF16))
        if l % 2 == 0:
            w.update(wt_in=w_in_even[e].T.astype(BF16), w_out=w_out_even[e].astype(BF16),
                     qk_gain=qk_norm_a[e], lam_w=diff_lambda[e], lam_init=0.8 - 0.6 * math.exp(-0.3 * l))
        else:
            w.update(wt_in=w_in_odd[e].T.astype(BF16), w_out=w_out_odd[e].astype(BF16), sink=sink_c[e])
        weights.append(w)
    tab = _position_table(max(x_prompt.shape[1], x_sample.shape[1]))
    return (_trunk(x_prompt, weights, tab), _trunk(x_sample, weights, tab))
```

```python
import functools
import math

import jax
import jax.numpy as jnp
from jax import lax
from jax.experimental import pallas as pl
from jax.experimental.pallas import tpu as pltpu

F32 = jnp.float32
BF16 = jnp.bfloat16

D_MODEL = 1024
HEAD_DIM = 64
GRID_W = 64
EPS = 1e-6
A_HEADS = 8
A_KV_HEADS = 2
AXIAL_THETA = 10000.0
B_HEADS = 4
B_V_DIM = 2 * HEAD_DIM
C_HEADS = 16
C_KV_HEADS = 4
WINDOW = 128
ROPE_THETA = 500000.0
ROPE_DIM = HEAD_DIM // 4
FFN_HIDDEN = 2816

A_Q = A_HEADS * HEAD_DIM
A_KV = A_KV_HEADS * HEAD_DIM
B_QK = B_HEADS * 2 * HEAD_DIM
B_V = B_HEADS * B_V_DIM
EVEN_IN = A_Q + 2 * A_KV + 2 * B_QK + B_V
C_Q = C_HEADS * HEAD_DIM
C_KV = C_KV_HEADS * HEAD_DIM
ODD_IN = C_Q + 2 * C_KV

LOG2E = 1.4426950408889634
Q_SCALE = HEAD_DIM ** -0.5 * LOG2E

TOKEN_TILE = 512
Q_TILE = 256
LANE_TILE = 256
WINDOW_TILES = 4
KEY_CHUNK = 256
STEPS_PER_ITER = 8
FFN_CHUNK = 256
VMEM_LIMIT = 56 * 1024 * 1024
K_LANES = 128
SUM_ROWS = 16
REF_ROW = HEAD_DIM
OVERFLOW_GUARD = 100.0

_TAB_ROWS = 80


def _rope_table(pos, dim, theta):
    inv = theta ** (-jnp.arange(0, dim, 2, dtype=F32) / dim)
    ang = pos.astype(F32)[:, None] * inv[None, :]
    return jnp.cos(ang), jnp.sin(ang)


def _position_table(seq):
    rows = seq // GRID_W
    t_row = jnp.broadcast_to(jnp.arange(rows)[:, None], (rows, GRID_W)).reshape(-1)
    t_col = jnp.broadcast_to(jnp.arange(GRID_W)[None, :], (rows, GRID_W)).reshape(-1)
    row_cos, row_sin = _rope_table(t_row, HEAD_DIM // 2, AXIAL_THETA)
    col_cos, col_sin = _rope_table(t_col, HEAD_DIM // 2, AXIAL_THETA)
    cos, sin = _rope_table(jnp.arange(seq), ROPE_DIM, ROPE_THETA)
    return jnp.concatenate([row_cos, row_sin, col_cos, col_sin, cos, sin], axis=1).T


def _rms(x, g):
    return x * lax.rsqrt(jnp.mean(x * x, axis=-1, keepdims=True) + EPS) * g


def _axial_rope_t(t, tab):
    rc, rs, cc, cs = tab[0:16][None], tab[16:32][None], tab[32:48][None], tab[48:64][None]
    x1, x2, x3, x4 = t[:, 0:16], t[:, 16:32], t[:, 32:48], t[:, 48:64]
    return jnp.concatenate(
        [x1 * rc - x2 * rs, x2 * rc + x1 * rs, x3 * cc - x4 * cs, x4 * cc + x3 * cs], axis=1)


def _partial_rope_t(t, tab):
    c, s = tab[64:72][None], tab[72:80][None]
    x1, x2 = t[:, 0:8], t[:, 8:16]
    return jnp.concatenate([x1 * c - x2 * s, x2 * c + x1 * s, t[:, 16:]], axis=1)


def _head_rms_t(t, g):
    r = lax.rsqrt(jnp.mean(t * t, axis=1, keepdims=True) + EPS)
    return t * r * g[None]


def _unit_row_block(heads, rows, n):
    return (lax.broadcasted_iota(jnp.int32, (heads, rows, n), 1) == 0).astype(F32)


def _with_ones_column(t):
    heads, d, n = t.shape
    aug = jnp.concatenate([t, _unit_row_block(heads, K_LANES - d, n)], axis=1)
    return aug.reshape(heads * K_LANES, n)


def _with_ones_rows(t):
    heads, dv, n = t.shape
    aug = jnp.concatenate([t, _unit_row_block(heads, SUM_ROWS, n)], axis=1)
    return aug.reshape(heads * (dv + SUM_ROWS), n)


def _proj_t(wt_ref, lo, hi, h):
    return lax.dot_general(wt_ref[lo:hi, :], h, (((1,), (1,)), ((), ())),
                           preferred_element_type=F32)


def _even_in_kernel(x_ref, g_ref, wt_ref, tab_ref, qkg_ref,
                    qa_ref, ka_ref, va_ref, qb_ref, kb_ref, vb_ref):
    n = x_ref.shape[0]
    h = _rms(x_ref[...], g_ref[...]).astype(BF16)
    tab = tab_ref[...]
    o = 0
    t = _proj_t(wt_ref, o, o + A_Q, h).reshape(A_HEADS, HEAD_DIM, n)
    t = _axial_rope_t(_head_rms_t(t, qkg_ref[0]), tab) * Q_SCALE
    qa_ref[...] = t.reshape(A_Q, n).astype(BF16)
    o += A_Q
    t = _proj_t(wt_ref, o, o + A_KV, h).reshape(A_KV_HEADS, HEAD_DIM, n)
    t = _axial_rope_t(_head_rms_t(t, qkg_ref[1]), tab)
    ka_ref[...] = _with_ones_column(t).T.astype(BF16)
    o += A_KV
    t = _proj_t(wt_ref, o, o + A_KV, h).reshape(A_KV_HEADS, HEAD_DIM, n)
    va_ref[...] = _with_ones_rows(t).astype(BF16)
    o += A_KV
    t = _proj_t(wt_ref, o, o + B_QK, h).reshape(2 * B_HEADS, HEAD_DIM, n)
    qb_ref[...] = (_partial_rope_t(t, tab) * Q_SCALE).reshape(B_QK, n).astype(BF16)
    o += B_QK
    t = _proj_t(wt_ref, o, o + B_QK, h).reshape(2 * B_HEADS, HEAD_DIM, n)
    kb_ref[...] = _with_ones_column(_partial_rope_t(t, tab)).T.astype(BF16)
    o += B_QK
    t = _proj_t(wt_ref, o, o + B_V, h).reshape(B_HEADS, B_V_DIM, n)
    vb_ref[...] = _with_ones_rows(t).astype(BF16)


def _odd_in_kernel(x_ref, g_ref, wt_ref, tab_ref, q_ref, k_ref, v_ref):
    n = x_ref.shape[0]
    h = _rms(x_ref[...], g_ref[...]).astype(BF16)
    tab = tab_ref[...]
    t = _proj_t(wt_ref, 0, C_Q, h).reshape(C_HEADS, HEAD_DIM, n)
    q_ref[...] = (_partial_rope_t(t, tab) * Q_SCALE).reshape(C_Q, n).astype(BF16)
    t = _proj_t(wt_ref, C_Q, C_Q + C_KV, h).reshape(C_KV_HEADS, HEAD_DIM, n)
    k_ref[...] = _with_ones_column(_partial_rope_t(t, tab)).T.astype(BF16)
    t = _proj_t(wt_ref, C_Q + C_KV, ODD_IN, h).reshape(C_KV_HEADS, HEAD_DIM, n)
    v_ref[...] = _with_ones_rows(t).astype(BF16)


def _resident(shape):
    nd = len(shape)
    return pl.BlockSpec(shape, lambda *_: (0,) * nd, pipeline_mode=pl.Buffered(1))


def _token_params():
    return pltpu.CompilerParams(dimension_semantics=("parallel",), vmem_limit_bytes=VMEM_LIMIT)


def _input_projection(x, gain, wt, tab, pos_tile, qk_gain=None):
    tokens = x.shape[0]
    tm = TOKEN_TILE
    grid = (tokens // tm,)
    in_specs = [
        pl.BlockSpec((tm, D_MODEL), lambda i: (i, 0)),
        _resident((1, D_MODEL)),
        _resident(wt.shape),
        pl.BlockSpec((_TAB_ROWS, tm), lambda i: (0, pos_tile(i))),
    ]
    args = [x, gain.reshape(1, D_MODEL), wt, tab]

    def feat(rows):
        return (jax.ShapeDtypeStruct((rows, tokens), BF16), pl.BlockSpec((rows, tm), lambda i: (0, i)))

    def tokm(cols):
        return (jax.ShapeDtypeStruct((tokens, cols), BF16), pl.BlockSpec((tm, cols), lambda i: (i, 0)))

    if qk_gain is not None:
        in_specs.append(_resident((2, HEAD_DIM, 1)))
        args.append(qk_gain.reshape(2, HEAD_DIM, 1))
        outs = [feat(A_Q), tokm(A_KV_HEADS * K_LANES), feat(A_KV_HEADS * (HEAD_DIM + SUM_ROWS)),
                feat(B_QK), tokm(2 * B_HEADS * K_LANES), feat(B_HEADS * (B_V_DIM + SUM_ROWS))]
        body, name = _even_in_kernel, "even_in_proj"
    else:
        outs = [feat(C_Q), tokm(C_KV_HEADS * K_LANES), feat(C_KV_HEADS * (HEAD_DIM + SUM_ROWS))]
        body, name = _odd_in_kernel, "odd_in_proj"
    return pl.pallas_call(
        body, grid=grid, in_specs=in_specs,
        out_specs=[o[1] for o in outs], out_shape=[o[0] for o in outs],
        compiler_params=_token_params(), name=name)(*args)


def _flash_t(k_refs, vt_ref, qaug_ref, s_a, s_b, acc_ref, key_chunk):
    n = qaug_ref.shape[1]
    width = n // len(k_refs)
    nc = k_refs[0].shape[0] // key_chunk
    assert nc >= 2 and STEPS_PER_ITER % 2 == 0 and STEPS_PER_ITER >= 2
    ref_rows = lax.broadcasted_iota(jnp.int32, (SUM_ROWS, n), 0) == 0

    def bf16_round(x):
        return x.astype(BF16).astype(F32)

    def set_reference(ref):
        qaug_ref[REF_ROW:REF_ROW + SUM_ROWS, :] = jnp.where(ref_rows, -ref, 0.0).astype(BF16)

    def tile_scores(t, off, rows, s_ref):
        lanes = slice(t * LANE_TILE, (t + 1) * LANE_TILE)
        k_ref = k_refs[t * LANE_TILE // width]
        s = jnp.dot(k_ref[pl.ds(off, rows), :], qaug_ref[:, lanes], preferred_element_type=F32)
        sb = s.astype(BF16)
        if s_ref is not None:
            s_ref[:, lanes] = sb
        return jnp.max(sb, axis=0, keepdims=True).astype(F32)

    def tile_accumulate(t, off, s_ref, alpha):
        lanes = slice(t * LANE_TILE, (t + 1) * LANE_TILE)
        p = jnp.exp2(s_ref[:, lanes])
        pv = jnp.dot(vt_ref[:, pl.ds(off, key_chunk)], p, preferred_element_type=F32)
        acc = acc_ref[:, lanes]
        acc_ref[:, lanes] = (acc if alpha is None else alpha[:, lanes] * acc) + pv

    def step(c_next, s_next, c_cur, s_cur, alpha):
        maxes = []
        for t in range(n // LANE_TILE):
            if c_next is not None:
                maxes.append(tile_scores(t, pl.multiple_of(c_next * key_chunk, key_chunk), key_chunk, s_next))
            if c_cur is not None:
                tile_accumulate(t, pl.multiple_of(c_cur * key_chunk, key_chunk), s_cur, alpha)
        return jnp.concatenate(maxes, axis=1) if maxes else None

    def sweep(top0):
        bufs = (s_a, s_b)
        acc_ref[...] = jnp.zeros(acc_ref.shape, F32)
        ref0 = bf16_round(top0)
        set_reference(ref0)
        mc0 = step(0, s_a, None, None, None)

        def group(k0, count, state, move_reference=True):
            ref_p, mc_p, top, over = state
            ref_g = ref_p
            if move_reference:
                ref_g = bf16_round(top)
                set_reference(ref_g)
            for u in range(count):
                alpha = jnp.exp2(ref_p - ref_g) if (u == 1 and move_reference) else None
                mc_k = step(k0 + u, bufs[(1 + u) % 2], k0 + u - 1, bufs[u % 2], alpha)
                top = jnp.maximum(top, (ref_p if u == 0 else ref_g) + mc_p)
                over = jnp.maximum(over, mc_p)
                mc_p = mc_k
            return ref_g, mc_p, top, over

        state = (ref0, mc0, top0, jnp.zeros((1, n), F32))
        iters = (nc - 1) // STEPS_PER_ITER
        state = lax.fori_loop(0, iters, lambda i, st: group(STEPS_PER_ITER * i + 1, STEPS_PER_ITER, st), state)
        rest = nc - 1 - iters * STEPS_PER_ITER
        if rest:
            state = group(iters * STEPS_PER_ITER + 1, rest, state, move_reference=rest > 1)
        ref_p, mc_p, top, over = state
        step(None, None, nc - 1, bufs[(nc - 1) % 2], None)
        return jnp.maximum(top, ref_p + mc_p), jnp.maximum(over, mc_p)

    set_reference(jnp.zeros((1, n), F32))
    first = jnp.concatenate([tile_scores(t, 0, K_LANES, None) for t in range(n // LANE_TILE)], axis=1)
    top, over = sweep(first)

    @pl.when(jnp.max(over) > OVERFLOW_GUARD)
    def _():
        sweep(top)


def _finish(acc_ref, dv):
    acc = acc_ref[...]
    return acc[:dv] / acc[dv:dv + 1]


def _heads_to_token_major(o, heads, tq):
    stacked = jnp.concatenate([o[:, g * tq:(g + 1) * tq] for g in range(heads)], axis=0)
    return stacked.T


def _fill_queries(qaug_ref, q_ref, heads, tq):
    n = qaug_ref.shape[1]
    for g in range(heads):
        qaug_ref[0:HEAD_DIM, g * tq:(g + 1) * tq] = q_ref[g * HEAD_DIM:(g + 1) * HEAD_DIM, :]
    qaug_ref[REF_ROW + SUM_ROWS:, :] = jnp.zeros((K_LANES - REF_ROW - SUM_ROWS, n), BF16)


def _attn_a_kernel(q_ref, k_ref, vt_ref, o_ref, qaug_ref, s_a, s_b, acc_ref, *, key_chunk):
    tq = q_ref.shape[1]
    group = A_HEADS // A_KV_HEADS
    _fill_queries(qaug_ref, q_ref, group, tq)
    _flash_t([k_ref], vt_ref, qaug_ref, s_a, s_b, acc_ref, key_chunk)
    o_ref[...] = _heads_to_token_major(_finish(acc_ref, HEAD_DIM), group, tq).astype(BF16)


def _attn_b_kernel(q_ref, k0_ref, k1_ref, vt_ref, lam_ref, o_ref, qaug_ref, s_a, s_b, acc_ref, *,
                   key_chunk, lam_init):
    tq = q_ref.shape[1]
    _fill_queries(qaug_ref, q_ref, 2, tq)
    _flash_t([k0_ref, k1_ref], vt_ref, qaug_ref, s_a, s_b, acc_ref, key_chunk)
    lf = lam_ref[...]
    lam = (jnp.exp(jnp.sum(lf[0:1] * lf[1:2], axis=1, keepdims=True))
           - jnp.exp(jnp.sum(lf[2:3] * lf[3:4], axis=1, keepdims=True)) + lam_init)
    o = _finish(acc_ref, B_V_DIM)
    o = o[:, :tq] - lam * o[:, tq:]
    o = o * lax.rsqrt(jnp.mean(o * o, axis=0, keepdims=True) + EPS) * (1.0 - lam_init)
    o_ref[...] = o.T.astype(BF16)


def _attn_c_kernel(sink_ref, q_ref, k_ref, vt_ref, o_ref, qaug_a, qaug_b, s_a, s_b, bias_ref):
    group = C_HEADS // C_KV_HEADS
    seq = k_ref.shape[0]
    tile = q_ref.shape[1] // WINDOW_TILES
    span = tile + 2 * WINDOW
    j = pl.program_id(1)
    q_base = pl.program_id(2) * q_ref.shape[1]
    rel0 = (lax.broadcasted_iota(jnp.int32, (span, tile), 0)
            - lax.broadcasted_iota(jnp.int32, (span, tile), 1))
    sinks = [jnp.full((1, tile), sink_ref[j * group + g] * LOG2E, F32) for g in range(group)]
    for qaug in (qaug_a, qaug_b):
        qaug[HEAD_DIM:, :] = jnp.zeros((K_LANES - HEAD_DIM, group * tile), BF16)

    def window(t):
        q0 = q_base + t * tile
        return q0, pl.multiple_of(jnp.clip(q0 - WINDOW, 0, seq - span), WINDOW)

    def tile_scores(t, g, qaug, s_ref):
        _, start = window(t)
        lanes = slice(g * tile, (g + 1) * tile)
        qaug[0:HEAD_DIM, lanes] = q_ref[g * HEAD_DIM:(g + 1) * HEAD_DIM, t * tile:(t + 1) * tile]
        s = jnp.dot(k_ref[pl.ds(start, span), :], qaug[:, lanes], preferred_element_type=F32)
        s = s + bias_ref[...]
        s_ref[:, lanes] = s
        return jnp.maximum(jnp.max(s, axis=0, keepdims=True), sinks[g])

    def tile_finish(t, g, s_ref, m):
        _, start = window(t)
        lanes = slice(g * tile, (g + 1) * tile)
        p = jnp.exp2(s_ref[:, lanes] - m).astype(BF16)
        acc = jnp.dot(vt_ref[:, pl.ds(start, span)], p, preferred_element_type=F32)
        l = acc[HEAD_DIM:HEAD_DIM + 1] + jnp.exp2(sinks[g] - m)
        return acc[:HEAD_DIM] / l

    bufs = ((qaug_a, s_a), (qaug_b, s_b))
    maxes = None
    for t in range(WINDOW_TILES + 1):
        new_maxes, outs = [], []
        if t < WINDOW_TILES:
            q0, start = window(t)
            in_band = jnp.abs(rel0 + (start - q0)) <= WINDOW
            bias_ref[...] = jnp.where(in_band, 0.0, -jnp.inf)
        for g in range(group):
            if t < WINDOW_TILES:
                new_maxes.append(tile_scores(t, g, *bufs[t % 2]))
            if t > 0:
                outs.append(tile_finish(t - 1, g, bufs[(t - 1) % 2][1], maxes[g]))
        if t > 0:
            o_ref[(t - 1) * tile:t * tile, :] = jnp.concatenate(outs, axis=0).T.astype(BF16)
        maxes = new_maxes


def _attn_params():
    return pltpu.CompilerParams(dimension_semantics=("parallel", "parallel", "arbitrary"),
                                vmem_limit_bytes=VMEM_LIMIT)


def _flash_scratch(key_chunk, n, dv):
    return [pltpu.VMEM((K_LANES, n), BF16), pltpu.VMEM((key_chunk, n), BF16),
            pltpu.VMEM((key_chunk, n), BF16), pltpu.VMEM((dv + SUM_ROWS, n), F32)]


def _attention_a(qt, k, vt, batch, seq):
    tq = 2 * Q_TILE
    nq = seq // tq
    group = A_HEADS // A_KV_HEADS
    width = group * HEAD_DIM
    key_chunk = min(KEY_CHUNK, seq // 2)
    return pl.pallas_call(
        functools.partial(_attn_a_kernel, key_chunk=key_chunk),
        grid=(batch, A_KV_HEADS, nq),
        scratch_shapes=_flash_scratch(key_chunk, group * tq, HEAD_DIM),
        in_specs=[
            pl.BlockSpec((width, tq), lambda b, j, i: (j, b * nq + i)),
            pl.BlockSpec((seq, K_LANES), lambda b, j, i: (b, j)),
            pl.BlockSpec((HEAD_DIM + SUM_ROWS, seq), lambda b, j, i: (j, b)),
        ],
        out_specs=pl.BlockSpec((tq, width), lambda b, j, i: (b * nq + i, j)),
        out_shape=jax.ShapeDtypeStruct((batch * seq, A_Q), BF16),
        compiler_params=_attn_params(), name="attn_axial")(qt, k, vt)


def _attention_b(qt, k, vt, lam_w, lam_init, batch, seq):
    tq = 4 * Q_TILE
    nq = seq // tq
    key_chunk = min(KEY_CHUNK, seq // 2)
    return pl.pallas_call(
        functools.partial(_attn_b_kernel, key_chunk=key_chunk, lam_init=lam_init),
        grid=(batch, B_HEADS, nq),
        scratch_shapes=_flash_scratch(key_chunk, 2 * tq, B_V_DIM),
        in_specs=[
            pl.BlockSpec((2 * HEAD_DIM, tq), lambda b, j, i: (j, b * nq + i)),
            pl.BlockSpec((seq, K_LANES), lambda b, j, i: (b, 2 * j)),
            pl.BlockSpec((seq, K_LANES), lambda b, j, i: (b, 2 * j + 1)),
            pl.BlockSpec((B_V_DIM + SUM_ROWS, seq), lambda b, j, i: (j, b)),
            pl.BlockSpec((4, HEAD_DIM), lambda b, j, i: (0, 0)),
        ],
        out_specs=pl.BlockSpec((tq, B_V_DIM), lambda b, j, i: (b * nq + i, j)),
        out_shape=jax.ShapeDtypeStruct((batch * seq, B_V), BF16),
        compiler_params=_attn_params(), name="attn_diff")(qt, k, k, vt, lam_w)


def _attention_c(qt, k, vt, sink, batch, seq):
    tq = WINDOW_TILES * Q_TILE
    nq = seq // tq
    group = C_HEADS // C_KV_HEADS
    width = group * HEAD_DIM
    span = Q_TILE + 2 * WINDOW
    return pl.pallas_call(
        _attn_c_kernel,
        grid=(batch, C_KV_HEADS, nq),
        scratch_shapes=[pltpu.VMEM((K_LANES, group * Q_TILE), BF16), pltpu.VMEM((K_LANES, group * Q_TILE), BF16),
                        pltpu.VMEM((span, group * Q_TILE), F32), pltpu.VMEM((span, group * Q_TILE), F32),
                        pltpu.VMEM((span, Q_TILE), F32)],
        in_specs=[
            pl.BlockSpec(memory_space=pltpu.SMEM),
            pl.BlockSpec((width, tq), lambda b, j, i: (j, b * nq + i)),
            pl.BlockSpec((seq, K_LANES), lambda b, j, i: (b, j)),
            pl.BlockSpec((HEAD_DIM + SUM_ROWS, seq), lambda b, j, i: (j, b)),
        ],
        out_specs=pl.BlockSpec((tq, width), lambda b, j, i: (b * nq + i, j)),
        out_shape=jax.ShapeDtypeStruct((batch * seq, C_Q), BF16),
        compiler_params=_attn_params(), name="attn_window")(sink, qt, k, vt)


def _post_kernel(lo_ref, hi_ref, x_ref, wo_ref, gmp_ref, gfp_ref, wgu_ref, wd_ref, gfo_ref, y_ref):
    half = lo_ref.shape[1]
    mix = (jnp.dot(lo_ref[...], wo_ref[0:half, :], preferred_element_type=F32)
           + jnp.dot(hi_ref[...], wo_ref[half:, :], preferred_element_type=F32))
    x = x_ref[...] + _rms(mix, gmp_ref[...])
    h = _rms(x, gfp_ref[...]).astype(BF16)
    acc = jnp.zeros(x.shape, F32)
    for c in range(FFN_HIDDEN // FFN_CHUNK):
        lo = c * FFN_CHUNK
        g = jnp.dot(h, wgu_ref[:, lo:lo + FFN_CHUNK], preferred_element_type=F32)
        u = jnp.dot(h, wgu_ref[:, FFN_HIDDEN + lo:FFN_HIDDEN + lo + FFN_CHUNK], preferred_element_type=F32)
        a = (g * (1.0 / (1.0 + jnp.exp(-g))) * u).astype(BF16)
        acc = acc + jnp.dot(a, wd_ref[lo:lo + FFN_CHUNK, :], preferred_element_type=F32)
    y_ref[...] = x + _rms(acc, gfo_ref[...])


def _post_attention(o_lo, o_hi, lo_blk, hi_blk, x, w_out, g_mix_post, g_ffn_pre, w_gu, w_down, g_ffn_post):
    tokens = x.shape[0]
    tm = TOKEN_TILE
    half = D_MODEL // 2
    row = lambda v: v.reshape(1, D_MODEL)
    return pl.pallas_call(
        _post_kernel, grid=(tokens // tm,),
        in_specs=[
            pl.BlockSpec((tm, half), lambda i: (i, lo_blk)),
            pl.BlockSpec((tm, half), lambda i: (i, hi_blk)),
            pl.BlockSpec((tm, D_MODEL), lambda i: (i, 0)),
            _resident(w_out.shape), _resident((1, D_MODEL)), _resident((1, D_MODEL)),
            _resident(w_gu.shape), _resident(w_down.shape), _resident((1, D_MODEL)),
        ],
        out_specs=pl.BlockSpec((tm, D_MODEL), lambda i: (i, 0)),
        out_shape=jax.ShapeDtypeStruct((tokens, D_MODEL), F32),
        compiler_params=_token_params(), name="out_proj_ffn")(
            o_lo, o_hi, x, w_out, row(g_mix_post), row(g_ffn_pre), w_gu, w_down, row(g_ffn_post))


def _trunk(x3, weights, tab):
    batch, seq, _ = x3.shape
    assert seq % TOKEN_TILE == 0 and seq % (4 * Q_TILE) == 0 and seq >= Q_TILE + 2 * WINDOW
    x = x3.reshape(batch * seq, D_MODEL)
    tiles_per_seq = seq // TOKEN_TILE
    pos_tile = lambda i: i % tiles_per_seq
    for l, w in enumerate(weights):
        if l % 2 == 0:
            qa, ka, va, qb, kb, vb = _input_projection(x, w["g_mix_pre"], w["wt_in"], tab, pos_tile, w["qk_gain"])
            o_lo = _attention_a(qa, ka, va, batch, seq)
            o_hi = _attention_b(qb, kb, vb, w["lam_w"], w["lam_init"], batch, seq)
            lo_blk, hi_blk = 0, 0
        else:
            q, k, v = _input_projection(x, w["g_mix_pre"], w["wt_in"], tab, pos_tile)
            o_lo = o_hi = _attention_c(q, k, v, w["sink"], batch, seq)
            lo_blk, hi_blk = 0, 1
        x = _post_attention(o_lo, o_hi, lo_blk, hi_blk, x, w["w_out"], w["g_mix_post"], w["g_ffn_pre"],
                            w["w_gu"], w["w_down"], w["g_ffn_post"])
    return x.reshape(batch, seq, D_MODEL)


def kernel(x_prompt, x_sample, w_in_even, w_out_even, qk_norm_a, diff_lambda, w_in_odd, w_out_odd,
           sink_c, w_gate_up, w_down, norm_mix_pre, norm_mix_post, norm_ffn_pre, norm_ffn_post):
    weights = []
    for l in range(w_gate_up.shape[0]):
        e = l // 2
        w = dict(g_mix_pre=norm_mix_pre[l], g_mix_post=norm_mix_post[l], g_ffn_pre=norm_ffn_pre[l],
                 g_ffn_post=norm_ffn_post[l], w_gu=w_gate_up[l].astype(BF16), w_down=w_down[l].astype(BF16))
        if l % 2 == 0:
            w.update(wt_in=w_in_even[e].T.astype(BF16), w_out=w_out_even[e].astype(BF16),
                     qk_gain=qk_norm_a[e], lam_w=diff_lambda[e], lam_init=0.8 - 0.6 * math.exp(-0.3 * l))
        else:
            w.update(wt_in=w_in_odd[e].T.astype(BF16), w_out=w_out_odd[e].astype(BF16), sink=sink_c[e])
        weights.append(w)
    tab = _position_table(max(x_prompt.shape[1], x_sample.shape[1]))
    return (_trunk(x_prompt, weights, tab), _trunk(x_sample, weights, tab))
```

```python
import functools
import math

import jax
import jax.numpy as jnp
from jax import lax
from jax.experimental import pallas as pl
from jax.experimental.pallas import tpu as pltpu

F32 = jnp.float32
BF16 = jnp.bfloat16

D_MODEL = 1024
HEAD_DIM = 64
GRID_W = 64
EPS = 1e-6
A_HEADS = 8
A_KV_HEADS = 2
AXIAL_THETA = 10000.0
B_HEADS = 4
B_V_DIM = 2 * HEAD_DIM
C_HEADS = 16
C_KV_HEADS = 4
WINDOW = 128
ROPE_THETA = 500000.0
ROPE_DIM = HEAD_DIM // 4
FFN_HIDDEN = 2816

A_Q = A_HEADS * HEAD_DIM
A_KV = A_KV_HEADS * HEAD_DIM
B_QK = B_HEADS * 2 * HEAD_DIM
B_V = B_HEADS * B_V_DIM
EVEN_IN = A_Q + 2 * A_KV + 2 * B_QK + B_V
C_Q = C_HEADS * HEAD_DIM
C_KV = C_KV_HEADS * HEAD_DIM
ODD_IN = C_Q + 2 * C_KV

LOG2E = 1.4426950408889634
Q_SCALE = HEAD_DIM ** -0.5 * LOG2E

TOKEN_TILE = 512
Q_TILE = 256
LANE_TILE = 256
WINDOW_TILES = 8
KEY_CHUNK = 256
STEPS_PER_ITER = 8
FFN_CHUNK = 256
VMEM_LIMIT = 56 * 1024 * 1024
K_LANES = 128
SUM_ROWS = 16
REF_ROW = HEAD_DIM
OVERFLOW_GUARD = 100.0

_TAB_ROWS = 80


def _rope_table(pos, dim, theta):
    inv = theta ** (-jnp.arange(0, dim, 2, dtype=F32) / dim)
    ang = pos.astype(F32)[:, None] * inv[None, :]
    return jnp.cos(ang), jnp.sin(ang)


def _position_table(seq):
    rows = seq // GRID_W
    t_row = jnp.broadcast_to(jnp.arange(rows)[:, None], (rows, GRID_W)).reshape(-1)
    t_col = jnp.broadcast_to(jnp.arange(GRID_W)[None, :], (rows, GRID_W)).reshape(-1)
    row_cos, row_sin = _rope_table(t_row, HEAD_DIM // 2, AXIAL_THETA)
    col_cos, col_sin = _rope_table(t_col, HEAD_DIM // 2, AXIAL_THETA)
    cos, sin = _rope_table(jnp.arange(seq), ROPE_DIM, ROPE_THETA)
    return jnp.concatenate([row_cos, row_sin, col_cos, col_sin, cos, sin], axis=1).T


def _rms(x, g):
    return x * lax.rsqrt(jnp.mean(x * x, axis=-1, keepdims=True) + EPS) * g


def _axial_rope_t(t, tab):
    rc, rs, cc, cs = tab[0:16][None], tab[16:32][None], tab[32:48][None], tab[48:64][None]
    x1, x2, x3, x4 = t[:, 0:16], t[:, 16:32], t[:, 32:48], t[:, 48:64]
    return jnp.concatenate(
        [x1 * rc - x2 * rs, x2 * rc + x1 * rs, x3 * cc - x4 * cs, x4 * cc + x3 * cs], axis=1)


def _partial_rope_t(t, tab):
    c, s = tab[64:72][None], tab[72:80][None]
    x1, x2 = t[:, 0:8], t[:, 8:16]
    return jnp.concatenate([x1 * c - x2 * s, x2 * c + x1 * s, t[:, 16:]], axis=1)


def _head_rms_t(t, g):
    r = lax.rsqrt(jnp.mean(t * t, axis=1, keepdims=True) + EPS)
    return t * r * g[None]


def _unit_row_block(heads, rows, n):
    return (lax.broadcasted_iota(jnp.int32, (heads, rows, n), 1) == 0).astype(F32)


def _with_ones_column(t):
    heads, d, n = t.shape
    aug = jnp.concatenate([t, _unit_row_block(heads, K_LANES - d, n)], axis=1)
    return aug.reshape(heads * K_LANES, n)


def _with_ones_rows(t):
    heads, dv, n = t.shape
    aug = jnp.concatenate([t, _unit_row_block(heads, SUM_ROWS, n)], axis=1)
    return aug.reshape(heads * (dv + SUM_ROWS), n)


def _proj_t(wt_ref, lo, hi, h):
    return lax.dot_general(wt_ref[lo:hi, :], h, (((1,), (1,)), ((), ())),
                           preferred_element_type=F32)


def _even_in_kernel(x_ref, g_ref, wt_ref, tab_ref, qkg_ref,
                    qa_ref, ka_ref, va_ref, qb_ref, kb_ref, vb_ref):
    n = x_ref.shape[0]
    h = _rms(x_ref[...], g_ref[...]).astype(BF16)
    tab = tab_ref[...]
    o = 0
    t = _proj_t(wt_ref, o, o + A_Q, h).reshape(A_HEADS, HEAD_DIM, n)
    t = _axial_rope_t(_head_rms_t(t, qkg_ref[0]), tab) * Q_SCALE
    qa_ref[...] = t.reshape(A_Q, n).astype(BF16)
    o += A_Q
    t = _proj_t(wt_ref, o, o + A_KV, h).reshape(A_KV_HEADS, HEAD_DIM, n)
    t = _axial_rope_t(_head_rms_t(t, qkg_ref[1]), tab)
    ka_ref[...] = _with_ones_column(t).T.astype(BF16)
    o += A_KV
    t = _proj_t(wt_ref, o, o + A_KV, h).reshape(A_KV_HEADS, HEAD_DIM, n)
    va_ref[...] = _with_ones_rows(t).astype(BF16)
    o += A_KV
    t = _proj_t(wt_ref, o, o + B_QK, h).reshape(2 * B_HEADS, HEAD_DIM, n)
    qb_ref[...] = (_partial_rope_t(t, tab) * Q_SCALE).reshape(B_QK, n).astype(BF16)
    o += B_QK
    t = _proj_t(wt_ref, o, o + B_QK, h).reshape(2 * B_HEADS, HEAD_DIM, n)
    kb_ref[...] = _with_ones_column(_partial_rope_t(t, tab)).T.astype(BF16)
    o += B_QK
    t = _proj_t(wt_ref, o, o + B_V, h).reshape(B_HEADS, B_V_DIM, n)
    vb_ref[...] = _with_ones_rows(t).astype(BF16)


def _odd_in_kernel(x_ref, g_ref, wt_ref, tab_ref, q_ref, k_ref, v_ref):
    n = x_ref.shape[0]
    h = _rms(x_ref[...], g_ref[...]).astype(BF16)
    tab = tab_ref[...]
    t = _proj_t(wt_ref, 0, C_Q, h).reshape(C_HEADS, HEAD_DIM, n)
    q_ref[...] = (_partial_rope_t(t, tab) * Q_SCALE).reshape(C_Q, n).astype(BF16)
    t = _proj_t(wt_ref, C_Q, C_Q + C_KV, h).reshape(C_KV_HEADS, HEAD_DIM, n)
    k_ref[...] = _with_ones_column(_partial_rope_t(t, tab)).T.astype(BF16)
    t = _proj_t(wt_ref, C_Q + C_KV, ODD_IN, h).reshape(C_KV_HEADS, HEAD_DIM, n)
    v_ref[...] = _with_ones_rows(t).astype(BF16)


def _resident(shape):
    nd = len(shape)
    return pl.BlockSpec(shape, lambda *_: (0,) * nd, pipeline_mode=pl.Buffered(1))


def _token_params():
    return pltpu.CompilerParams(dimension_semantics=("parallel",), vmem_limit_bytes=VMEM_LIMIT)


def _input_projection(x, gain, wt, tab, pos_tile, qk_gain=None):
    tokens = x.shape[0]
    tm = TOKEN_TILE
    grid = (tokens // tm,)
    in_specs = [
        pl.BlockSpec((tm, D_MODEL), lambda i: (i, 0)),
        _resident((1, D_MODEL)),
        _resident(wt.shape),
        pl.BlockSpec((_TAB_ROWS, tm), lambda i: (0, pos_tile(i))),
    ]
    args = [x, gain.reshape(1, D_MODEL), wt, tab]

    def feat(rows):
        return (jax.ShapeDtypeStruct((rows, tokens), BF16), pl.BlockSpec((rows, tm), lambda i: (0, i)))

    def tokm(cols):
        return (jax.ShapeDtypeStruct((tokens, cols), BF16), pl.BlockSpec((tm, cols), lambda i: (i, 0)))

    if qk_gain is not None:
        in_specs.append(_resident((2, HEAD_DIM, 1)))
        args.append(qk_gain.reshape(2, HEAD_DIM, 1))
        outs = [feat(A_Q), tokm(A_KV_HEADS * K_LANES), feat(A_KV_HEADS * (HEAD_DIM + SUM_ROWS)),
                feat(B_QK), tokm(2 * B_HEADS * K_LANES), feat(B_HEADS * (B_V_DIM + SUM_ROWS))]
        body, name = _even_in_kernel, "even_in_proj"
    else:
        outs = [feat(C_Q), tokm(C_KV_HEADS * K_LANES), feat(C_KV_HEADS * (HEAD_DIM + SUM_ROWS))]
        body, name = _odd_in_kernel, "odd_in_proj"
    return pl.pallas_call(
        body, grid=grid, in_specs=in_specs,
        out_specs=[o[1] for o in outs], out_shape=[o[0] for o in outs],
        compiler_params=_token_params(), name=name)(*args)


def _flash_t(k_refs, vt_ref, qaug_ref, s_a, s_b, acc_ref, key_chunk):
    n = qaug_ref.shape[1]
    width = n // len(k_refs)
    nc = k_refs[0].shape[0] // key_chunk
    assert nc >= 2 and STEPS_PER_ITER % 2 == 0 and STEPS_PER_ITER >= 2
    ref_rows = lax.broadcasted_iota(jnp.int32, (SUM_ROWS, n), 0) == 0

    def bf16_round(x):
        return x.astype(BF16).astype(F32)

    def set_reference(ref):
        qaug_ref[REF_ROW:REF_ROW + SUM_ROWS, :] = jnp.where(ref_rows, -ref, 0.0).astype(BF16)

    def tile_scores(t, off, rows, s_ref):
        lanes = slice(t * LANE_TILE, (t + 1) * LANE_TILE)
        k_ref = k_refs[t * LANE_TILE // width]
        s = jnp.dot(k_ref[pl.ds(off, rows), :], qaug_ref[:, lanes], preferred_element_type=F32)
        sb = s.astype(BF16)
        if s_ref is not None:
            s_ref[:, lanes] = sb
        return jnp.max(sb, axis=0, keepdims=True).astype(F32)

    def tile_accumulate(t, off, s_ref, alpha):
        lanes = slice(t * LANE_TILE, (t + 1) * LANE_TILE)
        p = jnp.exp2(s_ref[:, lanes])
        pv = jnp.dot(vt_ref[:, pl.ds(off, key_chunk)], p, preferred_element_type=F32)
        acc = acc_ref[:, lanes]
        acc_ref[:, lanes] = (acc if alpha is None else alpha[:, lanes] * acc) + pv

    def step(c_next, s_next, c_cur, s_cur, alpha):
        maxes = []
        for t in range(n // LANE_TILE):
            if c_next is not None:
                maxes.append(tile_scores(t, pl.multiple_of(c_next * key_chunk, key_chunk), key_chunk, s_next))
            if c_cur is not None:
                tile_accumulate(t, pl.multiple_of(c_cur * key_chunk, key_chunk), s_cur, alpha)
        return jnp.concatenate(maxes, axis=1) if maxes else None

    def sweep(top0):
        bufs = (s_a, s_b)
        acc_ref[...] = jnp.zeros(acc_ref.shape, F32)
        ref0 = bf16_round(top0)
        set_reference(ref0)
        mc0 = step(0, s_a, None, None, None)

        def group(k0, count, state, move_reference=True):
            ref_p, mc_p, top, over = state
            ref_g = ref_p
            if move_reference:
                ref_g = bf16_round(top)
                set_reference(ref_g)
            for u in range(count):
                alpha = jnp.exp2(ref_p - ref_g) if (u == 1 and move_reference) else None
                mc_k = step(k0 + u, bufs[(1 + u) % 2], k0 + u - 1, bufs[u % 2], alpha)
                top = jnp.maximum(top, (ref_p if u == 0 else ref_g) + mc_p)
                over = jnp.maximum(over, mc_p)
                mc_p = mc_k
            return ref_g, mc_p, top, over

        state = (ref0, mc0, top0, jnp.zeros((1, n), F32))
        iters = (nc - 1) // STEPS_PER_ITER
        state = lax.fori_loop(0, iters, lambda i, st: group(STEPS_PER_ITER * i + 1, STEPS_PER_ITER, st), state)
        rest = nc - 1 - iters * STEPS_PER_ITER
        if rest:
            state = group(iters * STEPS_PER_ITER + 1, rest, state, move_reference=rest > 1)
        ref_p, mc_p, top, over = state
        step(None, None, nc - 1, bufs[(nc - 1) % 2], None)
        return jnp.maximum(top, ref_p + mc_p), jnp.maximum(over, mc_p)

    set_reference(jnp.zeros((1, n), F32))
    first = jnp.concatenate([tile_scores(t, 0, K_LANES, None) for t in range(n // LANE_TILE)], axis=1)
    top, over = sweep(first)

    @pl.when(jnp.max(over) > OVERFLOW_GUARD)
    def _():
        sweep(top)


def _finish(acc_ref, dv):
    acc = acc_ref[...]
    return acc[:dv] / acc[dv:dv + 1]


def _heads_to_token_major(o, heads, tq):
    stacked = jnp.concatenate([o[:, g * tq:(g + 1) * tq] for g in range(heads)], axis=0)
    return stacked.T


def _fill_queries(qaug_ref, q_ref, heads, tq):
    n = qaug_ref.shape[1]
    for g in range(heads):
        qaug_ref[0:HEAD_DIM, g * tq:(g + 1) * tq] = q_ref[g * HEAD_DIM:(g + 1) * HEAD_DIM, :]
    qaug_ref[REF_ROW + SUM_ROWS:, :] = jnp.zeros((K_LANES - REF_ROW - SUM_ROWS, n), BF16)


def _attn_a_kernel(q_ref, k_ref, vt_ref, o_ref, qaug_ref, s_a, s_b, acc_ref, *, key_chunk):
    tq = q_ref.shape[1]
    group = A_HEADS // A_KV_HEADS
    _fill_queries(qaug_ref, q_ref, group, tq)
    _flash_t([k_ref], vt_ref, qaug_ref, s_a, s_b, acc_ref, key_chunk)
    o_ref[...] = _heads_to_token_major(_finish(acc_ref, HEAD_DIM), group, tq).astype(BF16)


def _attn_b_kernel(q_ref, k0_ref, k1_ref, vt_ref, lam_ref, o_ref, qaug_ref, s_a, s_b, acc_ref, *,
                   key_chunk, lam_init):
    tq = q_ref.shape[1]
    _fill_queries(qaug_ref, q_ref, 2, tq)
    _flash_t([k0_ref, k1_ref], vt_ref, qaug_ref, s_a, s_b, acc_ref, key_chunk)
    lf = lam_ref[...]
    lam = (jnp.exp(jnp.sum(lf[0:1] * lf[1:2], axis=1, keepdims=True))
           - jnp.exp(jnp.sum(lf[2:3] * lf[3:4], axis=1, keepdims=True)) + lam_init)
    o = _finish(acc_ref, B_V_DIM)
    o = o[:, :tq] - lam * o[:, tq:]
    o = o * lax.rsqrt(jnp.mean(o * o, axis=0, keepdims=True) + EPS) * (1.0 - lam_init)
    o_ref[...] = o.T.astype(BF16)


def _attn_c_kernel(sink_ref, q_ref, k_ref, vt_ref, o_ref, qaug_a, qaug_b, s_a, s_b, bias_ref):
    group = C_HEADS // C_KV_HEADS
    seq = k_ref.shape[0]
    tile = q_ref.shape[1] // WINDOW_TILES
    span = tile + 2 * WINDOW
    j = pl.program_id(1)
    q_base = pl.program_id(2) * q_ref.shape[1]
    rel0 = (lax.broadcasted_iota(jnp.int32, (span, tile), 0)
            - lax.broadcasted_iota(jnp.int32, (span, tile), 1))
    sinks = [jnp.full((1, tile), sink_ref[j * group + g] * LOG2E, F32) for g in range(group)]
    for qaug in (qaug_a, qaug_b):
        qaug[HEAD_DIM:, :] = jnp.zeros((K_LANES - HEAD_DIM, group * tile), BF16)

    def window(t):
        q0 = q_base + t * tile
        return q0, pl.multiple_of(jnp.clip(q0 - WINDOW, 0, seq - span), WINDOW)

    def tile_scores(t, g, qaug, s_ref):
        _, start = window(t)
        lanes = slice(g * tile, (g + 1) * tile)
        qaug[0:HEAD_DIM, lanes] = q_ref[g * HEAD_DIM:(g + 1) * HEAD_DIM, t * tile:(t + 1) * tile]
        s = jnp.dot(k_ref[pl.ds(start, span), :], qaug[:, lanes], preferred_element_type=F32)
        s = s + bias_ref[...]
        s_ref[:, lanes] = s
        return jnp.maximum(jnp.max(s, axis=0, keepdims=True), sinks[g])

    def tile_finish(t, g, s_ref, m):
        _, start = window(t)
        lanes = slice(g * tile, (g + 1) * tile)
        p = jnp.exp2(s_ref[:, lanes] - m).astype(BF16)
        acc = jnp.dot(vt_ref[:, pl.ds(start, span)], p, preferred_element_type=F32)
        l = acc[HEAD_DIM:HEAD_DIM + 1] + jnp.exp2(sinks[g] - m)
        return acc[:HEAD_DIM] / l

    bufs = ((qaug_a, s_a), (qaug_b, s_b))
    maxes = None
    for t in range(WINDOW_TILES + 1):
        new_maxes, outs = [], []
        if t <= 1 or t == WINDOW_TILES - 1:
            q0, start = window(t)
            in_band = jnp.abs(rel0 + (start - q0)) <= WINDOW
            bias_ref[...] = jnp.where(in_band, 0.0, -jnp.inf)
        for g in range(group):
            if t < WINDOW_TILES:
                new_maxes.append(tile_scores(t, g, *bufs[t % 2]))
            if t > 0:
                outs.append(tile_finish(t - 1, g, bufs[(t - 1) % 2][1], maxes[g]))
        if t > 0:
            o_ref[(t - 1) * tile:t * tile, :] = jnp.concatenate(outs, axis=0).T.astype(BF16)
        maxes = new_maxes


def _attn_params():
    return pltpu.CompilerParams(dimension_semantics=("parallel", "parallel", "arbitrary"),
                                vmem_limit_bytes=VMEM_LIMIT)


def _flash_scratch(key_chunk, n, dv):
    return [pltpu.VMEM((K_LANES, n), BF16), pltpu.VMEM((key_chunk, n), BF16),
            pltpu.VMEM((key_chunk, n), BF16), pltpu.VMEM((dv + SUM_ROWS, n), F32)]


def _attention_a(qt, k, vt, batch, seq):
    tq = 2 * Q_TILE
    nq = seq // tq
    group = A_HEADS // A_KV_HEADS
    width = group * HEAD_DIM
    key_chunk = min(KEY_CHUNK, seq // 2)
    return pl.pallas_call(
        functools.partial(_attn_a_kernel, key_chunk=key_chunk),
        grid=(batch, A_KV_HEADS, nq),
        scratch_shapes=_flash_scratch(key_chunk, group * tq, HEAD_DIM),
        in_specs=[
            pl.BlockSpec((width, tq), lambda b, j, i: (j, b * nq + i)),
            pl.BlockSpec((seq, K_LANES), lambda b, j, i: (b, j)),
            pl.BlockSpec((HEAD_DIM + SUM_ROWS, seq), lambda b, j, i: (j, b)),
        ],
        out_specs=pl.BlockSpec((tq, width), lambda b, j, i: (b * nq + i, j)),
        out_shape=jax.ShapeDtypeStruct((batch * seq, A_Q), BF16),
        compiler_params=_attn_params(), name="attn_axial")(qt, k, vt)


def _attention_b(qt, k, vt, lam_w, lam_init, batch, seq):
    tq = 4 * Q_TILE
    nq = seq // tq
    key_chunk = min(KEY_CHUNK, seq // 2)
    return pl.pallas_call(
        functools.partial(_attn_b_kernel, key_chunk=key_chunk, lam_init=lam_init),
        grid=(batch, B_HEADS, nq),
        scratch_shapes=_flash_scratch(key_chunk, 2 * tq, B_V_DIM),
        in_specs=[
            pl.BlockSpec((2 * HEAD_DIM, tq), lambda b, j, i: (j, b * nq + i)),
            pl.BlockSpec((seq, K_LANES), lambda b, j, i: (b, 2 * j)),
            pl.BlockSpec((seq, K_LANES), lambda b, j, i: (b, 2 * j + 1)),
            pl.BlockSpec((B_V_DIM + SUM_ROWS, seq), lambda b, j, i: (j, b)),
            pl.BlockSpec((4, HEAD_DIM), lambda b, j, i: (0, 0)),
        ],
        out_specs=pl.BlockSpec((tq, B_V_DIM), lambda b, j, i: (b * nq + i, j)),
        out_shape=jax.ShapeDtypeStruct((batch * seq, B_V), BF16),
        compiler_params=_attn_params(), name="attn_diff")(qt, k, k, vt, lam_w)


def _attention_c(qt, k, vt, sink, batch, seq):
    tq = WINDOW_TILES * Q_TILE
    nq = seq // tq
    group = C_HEADS // C_KV_HEADS
    width = group * HEAD_DIM
    span = Q_TILE + 2 * WINDOW
    return pl.pallas_call(
        _attn_c_kernel,
        grid=(batch, C_KV_HEADS, nq),
        scratch_shapes=[pltpu.VMEM((K_LANES, group * Q_TILE), BF16), pltpu.VMEM((K_LANES, group * Q_TILE), BF16),
                        pltpu.VMEM((span, group * Q_TILE), F32), pltpu.VMEM((span, group * Q_TILE), F32),
                        pltpu.VMEM((span, Q_TILE), F32)],
        in_specs=[
            pl.BlockSpec(memory_space=pltpu.SMEM),
            pl.BlockSpec((width, tq), lambda b, j, i: (j, b * nq + i)),
            pl.BlockSpec((seq, K_LANES), lambda b, j, i: (b, j)),
            pl.BlockSpec((HEAD_DIM + SUM_ROWS, seq), lambda b, j, i: (j, b)),
        ],
        out_specs=pl.BlockSpec((tq, width), lambda b, j, i: (b * nq + i, j)),
        out_shape=jax.ShapeDtypeStruct((batch * seq, C_Q), BF16),
        compiler_params=_attn_params(), name="attn_window")(sink, qt, k, vt)


def _post_kernel(lo_ref, hi_ref, x_ref, wo_ref, gmp_ref, gfp_ref, wgu_ref, wd_ref, gfo_ref, y_ref):
    half = lo_ref.shape[1]
    mix = (jnp.dot(lo_ref[...], wo_ref[0:half, :], preferred_element_type=F32)
           + jnp.dot(hi_ref[...], wo_ref[half:, :], preferred_element_type=F32))
    x = x_ref[...] + _rms(mix, gmp_ref[...])
    h = _rms(x, gfp_ref[...]).astype(BF16)
    acc = jnp.zeros(x.shape, F32)
    for c in range(FFN_HIDDEN // FFN_CHUNK):
        lo = c * FFN_CHUNK
        g = jnp.dot(h, wgu_ref[:, lo:lo + FFN_CHUNK], preferred_element_type=F32)
        u = jnp.dot(h, wgu_ref[:, FFN_HIDDEN + lo:FFN_HIDDEN + lo + FFN_CHUNK], preferred_element_type=F32)
        a = (g * (1.0 / (1.0 + jnp.exp(-g))) * u).astype(BF16)
        acc = acc + jnp.dot(a, wd_ref[lo:lo + FFN_CHUNK, :], preferred_element_type=F32)
    y_ref[...] = x + _rms(acc, gfo_ref[...])


def _post_attention(o_lo, o_hi, lo_blk, hi_blk, x, w_out, g_mix_post, g_ffn_pre, w_gu, w_down, g_ffn_post):
    tokens = x.shape[0]
    tm = TOKEN_TILE
    half = D_MODEL // 2
    row = lambda v: v.reshape(1, D_MODEL)
    return pl.pallas_call(
        _post_kernel, grid=(tokens // tm,),
        in_specs=[
            pl.BlockSpec((tm, half), lambda i: (i, lo_blk)),
            pl.BlockSpec((tm, half), lambda i: (i, hi_blk)),
            pl.BlockSpec((tm, D_MODEL), lambda i: (i, 0)),
            _resident(w_out.shape), _resident((1, D_MODEL)), _resident((1, D_MODEL)),
            _resident(w_gu.shape), _resident(w_down.shape), _resident((1, D_MODEL)),
        ],
        out_specs=pl.BlockSpec((tm, D_MODEL), lambda i: (i, 0)),
        out_shape=jax.ShapeDtypeStruct((tokens, D_MODEL), F32),
        compiler_params=_token_params(), name="out_proj_ffn")(
            o_lo, o_hi, x, w_out, row(g_mix_post), row(g_ffn_pre), w_gu, w_down, row(g_ffn_post))


def _trunk(x3, weights, tab):
    batch, seq, _ = x3.shape
    assert seq % TOKEN_TILE == 0 and seq % (max(4, WINDOW_TILES) * Q_TILE) == 0 and seq >= Q_TILE + 2 * WINDOW
    x = x3.reshape(batch * seq, D_MODEL)
    tiles_per_seq = seq // TOKEN_TILE
    pos_tile = lambda i: i % tiles_per_seq
    for l, w in enumerate(weights):
        if l % 2 == 0:
            qa, ka, va, qb, kb, vb = _input_projection(x, w["g_mix_pre"], w["wt_in"], tab, pos_tile, w["qk_gain"])
            o_lo = _attention_a(qa, ka, va, batch, seq)
            o_hi = _attention_b(qb, kb, vb, w["lam_w"], w["lam_init"], batch, seq)
            lo_blk, hi_blk = 0, 0
        else:
            q, k, v = _input_projection(x, w["g_mix_pre"], w["wt_in"], tab, pos_tile)
            o_lo = o_hi = _attention_c(q, k, v, w["sink"], batch, seq)
            lo_blk, hi_blk = 0, 1
        x = _post_attention(o_lo, o_hi, lo_blk, hi_blk, x, w["w_out"], w["g_mix_post"], w["g_ffn_pre"],
                            w["w_gu"], w["w_down"], w["g_ffn_post"])
    return x.reshape(batch, seq, D_MODEL)


def kernel(x_prompt, x_sample, w_in_even, w_out_even, qk_norm_a, diff_lambda, w_in_odd, w_out_odd,
           sink_c, w_gate_up, w_down, norm_mix_pre, norm_mix_post, norm_ffn_pre, norm_ffn_post):
    weights = []
    for l in range(w_gate_up.shape[0]):
        e = l // 2
        w = dict(g_mix_pre=norm_mix_pre[l], g_mix_post=norm_mix_post[l], g_ffn_pre=norm_ffn_pre[l],
                 g_ffn_post=norm_ffn_post[l], w_gu=w_gate_up[l].astype(BF16), w_down=w_down[l].astype(BF16))
        if l % 2 == 0:
            w.update(wt_in=w_in_even[e].T.astype(BF16), w_out=w_out_even[e].astype(BF16),
                     qk_gain=qk_norm_a[e], lam_w=diff_lambda[e], lam_init=0.8 - 0.6 * math.exp(-0.3 * l))
        else:
            w.update(wt_in=w_in_odd[e].T.astype(BF16), w_out=w_out_odd[e].astype(BF16), sink=sink_c[e])
        weights.append(w)
    tab = _position_table(max(x_prompt.shape[1], x_sample.shape[1]))
    return (_trunk(x_prompt, weights, tab), _trunk(x_sample, weights, tab))
```
